```python
import jax
import jax.numpy as jnp
from jax import lax
import numpy as np

D_MODEL = 1024
BATCH = 4
SEQ = 8192
DEPTH = 2

N_GROUPS = 4
GROUP_WIDTH = D_MODEL // N_GROUPS
D_MIX = N_GROUPS * GROUP_WIDTH
HEAD_DIM = 64
CHUNK = 64
EPS = 1e-6
MASK_VALUE = -1e30
MIN_POS = 1e-30

HA = GROUP_WIDTH // HEAD_DIM
DK_A = HEAD_DIM
DV_A = HEAD_DIM
HB = GROUP_WIDTH // HEAD_DIM
DK_B = HEAD_DIM // 2
DV_B = HEAD_DIM
GLA_GATE_RANK = 16
GLA_GATE_NORMALIZER = 16.0
HC = GROUP_WIDTH // HEAD_DIM
MLA_Q_RANK = 256
MLA_KV_RANK = 128
MLA_NOPE = 64
MLA_ROPE = 32
MLA_V = HEAD_DIM
ROPE_THETA = 10000.0
Q_BLOCK = 128
HD = GROUP_WIDTH // HEAD_DIM
DK_D = HEAD_DIM
DV_D = HEAD_DIM
GDN_CONV = 4
D_FF = 2816
FFN_CONV = 3

IN_SPLITS = (
    HA * DK_A, HA * DK_A, HA * DV_A, HA * DV_A,
    HB * DK_B, HB * DK_B, HB * DV_B, GLA_GATE_RANK, HB * DV_B,
    MLA_Q_RANK, MLA_KV_RANK, MLA_ROPE,
    HD * DK_D, HD * DK_D, HD * DV_D, HD, HD, HD * DV_D,
)
D_IN = sum(IN_SPLITS)

kernel_name = 'hybrid_parallel_head_groups_block'


def rms_norm(x, g):
    xf = x.astype(jnp.float32)
    y = xf * lax.rsqrt(jnp.mean(xf * xf, axis=-1, keepdims=True) + EPS)
    return (y * g).astype(x.dtype)


def l2_norm(x):
    xf = x.astype(jnp.float32)
    return xf * lax.rsqrt(jnp.sum(xf * xf, axis=-1, keepdims=True) + EPS)


def heads(t, h, d):
    return t.reshape(t.shape[:-1] + (h, d))


def split_columns(z):
    out = []
    start = 0
    for w in IN_SPLITS:
        out.append(z[..., start:start + w])
        start += w
    return out


def causal_dwconv(x, w):
    k, c = w.shape
    return lax.conv_general_dilated(
        x, w[:, None, :].astype(x.dtype), window_strides=(1,), padding=[(k - 1, 0)],
        dimension_numbers=('NWC', 'WIO', 'NWC'), feature_group_count=c)


def rope_tables(s):
    inv = ROPE_THETA ** (-jnp.arange(0, MLA_ROPE, 2, dtype=jnp.float32) / MLA_ROPE)
    ang = jnp.arange(s, dtype=jnp.float32)[:, None] * inv[None, :]
    return jnp.cos(ang), jnp.sin(ang)


def apply_rope(x, cos, sin):
    xf = x.astype(jnp.float32)
    x1, x2 = xf[..., :MLA_ROPE // 2], xf[..., MLA_ROPE // 2:]
    return jnp.concatenate([x1 * cos - x2 * sin, x2 * cos + x1 * sin], axis=-1).astype(x.dtype)


def chunk_gla(q, k, v, log_f):
    bsz, s, h, dk = q.shape
    dv = v.shape[-1]
    n = s // CHUNK

    def to_chunks(t):
        return t.astype(jnp.float32).reshape(bsz, n, CHUNK, h, t.shape[-1]).transpose(1, 0, 3, 2, 4)

    qc, kc, vc, gc = to_chunks(q), to_chunks(k), to_chunks(v), to_chunks(log_f)
    bc = jnp.cumsum(gc, axis=3)
    causal = jnp.tril(jnp.ones((CHUNK, CHUNK), bool))[:, :, None]

    def step(state, inp):
        q_, k_, v_, b_ = inp
        diff = b_[:, :, :, None, :] - b_[:, :, None, :, :]
        decay = jnp.where(causal, jnp.exp(jnp.where(causal, diff, 0.0)), 0.0)
        attn = jnp.einsum('bhid,bhjd,bhijd->bhij', q_, k_, decay)
        b_last = b_[:, :, -1, :]
        o = (jnp.einsum('bhij,bhjv->bhiv', attn, v_)
             + jnp.einsum('bhid,bhdv->bhiv', q_ * jnp.exp(b_), state))
        state = (jnp.exp(b_last)[..., None] * state
                 + jnp.einsum('bhjd,bhjv->bhdv', k_ * jnp.exp(b_last[:, :, None, :] - b_), v_))
        return state, o

    s0 = jnp.zeros((bsz, h, dk, dv), jnp.float32)
    _, o = lax.scan(step, s0, (qc, kc, vc, bc))
    return o.transpose(1, 0, 3, 2, 4).reshape(bsz, s, h, dv)


def chunk_gated_delta(q, k, v, beta, log_g):
    bsz, s, h, dk = q.shape
    dv = v.shape[-1]
    n = s // CHUNK

    def to_chunks(t):
        t = t.astype(jnp.float32).reshape((bsz, n, CHUNK, h) + t.shape[3:])
        return jnp.moveaxis(t, (1, 3), (0, 2))

    qc, kc, vc = to_chunks(q), to_chunks(k), to_chunks(v)
    bt, gc = to_chunks(beta), to_chunks(log_g)
    b = jnp.cumsum(gc, axis=-1)
    incl = jnp.tril(jnp.ones((CHUNK, CHUNK), bool))
    strict = jnp.tril(jnp.ones((CHUNK, CHUNK), bool), -1)
    diff = b[..., :, None] - b[..., None, :]
    lmask = jnp.where(incl, jnp.exp(jnp.where(incl, diff, 0.0)), 0.0)
    kb = kc * bt[..., None]
    m = jnp.where(strict, jnp.einsum('nbhid,nbhjd->nbhij', kb, kc) * lmask, 0.0)
    eye = jnp.eye(CHUNK, dtype=jnp.float32)
    rhs = jnp.concatenate([vc * bt[..., None], kb * jnp.exp(b)[..., None]], axis=-1)
    sol = lax.linalg.triangular_solve(m + eye, rhs, left_side=True, lower=True, unit_diagonal=True)
    u, w = sol[..., :dv], sol[..., dv:]
    a_qk = jnp.einsum('nbhid,nbhjd->nbhij', qc, kc) * lmask

    def step(state, inp):
        q_, k_, u_, w_, a_, b_ = inp
        v_new = u_ - jnp.einsum('bhcd,bhdv->bhcv', w_, state)
        o = (jnp.einsum('bhcd,bhdv->bhcv', q_ * jnp.exp(b_)[..., None], state)
             + jnp.einsum('bhij,bhjv->bhiv', a_, v_new))
        b_last = b_[..., -1:]
        state = (jnp.exp(b_last)[..., None] * state
                 + jnp.einsum('bhcd,bhcv->bhdv', k_ * jnp.exp(b_last - b_)[..., None], v_new))
        return state, o

    s0 = jnp.zeros((bsz, h, dk, dv), jnp.float32)
    _, o = lax.scan(step, s0, (qc, kc, u, w, a_qk, b))
    return jnp.moveaxis(o, (0, 2), (1, 3)).reshape(bsz, s, h, dv)


def mla_attention(q_nope, q_rope, k_nope, k_rope, v):
    bsz, s, h, _ = q_nope.shape
    nb = s // Q_BLOCK
    scale = (MLA_NOPE + MLA_ROPE) ** -0.5
    key_pos = jnp.arange(s)

    def blocks(t):
        return jnp.moveaxis(t.reshape((bsz, nb, Q_BLOCK) + t.shape[2:]), 1, 0)

    def attend(args):
        qn, qr, blk = args
        sc = (jnp.einsum('bqhd,bkhd->bhqk', qn, k_nope)
              + jnp.einsum('bqhd,bkd->bhqk', qr, k_rope)).astype(jnp.float32) * scale
        q_pos = blk * Q_BLOCK + jnp.arange(Q_BLOCK)
        mask = key_pos[None, :] <= q_pos[:, None]
        p = jax.nn.softmax(jnp.where(mask, sc, MASK_VALUE), axis=-1)
        return jnp.einsum('bhqk,bkhv->bqhv', p.astype(v.dtype), v)

    o = lax.map(attend, (blocks(q_nope), blocks(q_rope), jnp.arange(nb)))
    return jnp.moveaxis(o, 0, 1).reshape(bsz, s, h, v.shape[-1])


def hgrn_lower_bounds(lb_logits):
    p = jax.nn.softmax(lb_logits.astype(jnp.float32), axis=0)
    return jnp.cumsum(p, axis=0) - p[0]


def hgrn2_mixer(q, f_logit, i, g, lb, norm_g):
    dtype = q.dtype
    zf = f_logit.astype(jnp.float32)
    f = lb + (1.0 - lb) * jax.nn.sigmoid(zf)
    log_f = jnp.log(jnp.maximum(f, MIN_POS))
    k = (1.0 - lb) * jax.nn.sigmoid(-zf)
    qh = heads(jax.nn.silu(q), HA, DK_A) * DK_A ** -0.5
    o = chunk_gla(qh, heads(k, HA, DK_A), heads(i, HA, DV_A), heads(log_f, HA, DK_A))
    o = rms_norm(o, norm_g) * jax.nn.sigmoid(heads(g, HA, DV_A).astype(jnp.float32))
    return o.reshape(o.shape[:2] + (HA * DV_A,)).astype(dtype)


def gla_mixer(q, k, v, gate_code, g, w_gk2, b_gk, norm_g):
    dtype = q.dtype
    log_gk = jax.nn.log_sigmoid((gate_code @ w_gk2 + b_gk).astype(jnp.float32)) / GLA_GATE_NORMALIZER
    o = chunk_gla(heads(q, HB, DK_B) * DK_B ** -0.5, heads(k, HB, DK_B), heads(v, HB, DV_B),
                  heads(log_gk, HB, DK_B))
    o = rms_norm(o, norm_g) * jax.nn.silu(heads(g, HB, DV_B).astype(jnp.float32))
    return o.reshape(o.shape[:2] + (HB * DV_B,)).astype(dtype)


def mla_mixer(c_q, c_kv, k_rope, q_norm_g, w_uq, kv_norm_g, w_ukv, cos, sin):
    dtype = c_q.dtype
    q = heads(rms_norm(c_q, q_norm_g) @ w_uq, HC, MLA_NOPE + MLA_ROPE)
    kv = heads(rms_norm(c_kv, kv_norm_g) @ w_ukv, HC, MLA_NOPE + MLA_V)
    q_nope = q[..., :MLA_NOPE]
    q_rope = apply_rope(q[..., MLA_NOPE:], cos[:, None, :], sin[:, None, :])
    k_nope, v = kv[..., :MLA_NOPE], kv[..., MLA_NOPE:]
    k_rope = apply_rope(k_rope, cos, sin)
    o = mla_attention(q_nope, q_rope, k_nope, k_rope, v)
    return o.reshape(o.shape[:2] + (HC * MLA_V,)).astype(dtype)


def gdn_mixer(q, k, v, beta_logit, a, z, conv_w, a_log, dt_bias, norm_g):
    dtype = q.dtype
    qkv = jax.nn.silu(causal_dwconv(jnp.concatenate([q, k, v], axis=-1), conv_w))
    qd = l2_norm(heads(qkv[..., :HD * DK_D], HD, DK_D)) * DK_D ** -0.5
    kd = l2_norm(heads(qkv[..., HD * DK_D:2 * HD * DK_D], HD, DK_D))
    vd = heads(qkv[..., 2 * HD * DK_D:], HD, DV_D)
    beta = jax.nn.sigmoid(beta_logit.astype(jnp.float32))
    log_g = -jnp.exp(a_log.astype(jnp.float32)) * jax.nn.softplus(a.astype(jnp.float32) + dt_bias)
    o = chunk_gated_delta(qd, kd, vd, beta, log_g)
    o = rms_norm(o, norm_g) * jax.nn.silu(heads(z, HD, DV_D).astype(jnp.float32))
    return o.reshape(o.shape[:2] + (HD * DV_D,)).astype(dtype)


def conv_glu_ffn(h, w_gate, w_up, conv_w, w_down):
    gate = causal_dwconv(h @ w_gate, conv_w)
    return (jax.nn.gelu(gate, approximate=True) * (h @ w_up)) @ w_down


def setup_inputs(seed: int = 0) -> dict:
    key = jax.random.key(seed)
    ks = jax.random.split(key, 26)

    def nrm(k, shape, scale):
        return jax.random.normal(k, shape, jnp.float32) * scale

    def gain(k, shape):
        return 1.0 + 0.1 * jax.random.normal(k, shape, jnp.float32)

    dt = jnp.exp(jax.random.uniform(ks[19], (DEPTH, HD), jnp.float32,
                                    jnp.log(1e-3), jnp.log(1e-1)))
    return {
        'x': jax.random.normal(ks[0], (BATCH, SEQ, D_MODEL), jnp.float32),
        'w_in': nrm(ks[1], (DEPTH, D_MODEL, D_IN), D_MODEL ** -0.5),
        'w_out': nrm(ks[2], (DEPTH, D_MIX, D_MODEL), D_MIX ** -0.5),
        'pre_mix_g': gain(ks[3], (DEPTH, D_MODEL)),
        'post_mix_g': gain(ks[4], (DEPTH, D_MODEL)),
        'pre_ffn_g': gain(ks[5], (DEPTH, D_MODEL)),
        'post_ffn_g': gain(ks[6], (DEPTH, D_MODEL)),
        'hgrn_lb_logits': nrm(ks[7], (DEPTH, HA * DK_A), 1.0),
        'hgrn_norm_g': gain(ks[8], (DEPTH, DV_A)),
        'gla_w_gk2': nrm(ks[9], (DEPTH, GLA_GATE_RANK, HB * DK_B), GLA_GATE_RANK ** -0.5),
        'gla_b_gk': nrm(ks[10], (DEPTH, HB * DK_B), 0.1),
        'gla_norm_g': gain(ks[11], (DEPTH, DV_B)),
        'mla_q_norm_g': gain(ks[12], (DEPTH, MLA_Q_RANK)),
        'mla_w_uq': nrm(ks[13], (DEPTH, MLA_Q_RANK, HC * (MLA_NOPE + MLA_ROPE)), MLA_Q_RANK ** -0.5),
        'mla_kv_norm_g': gain(ks[14], (DEPTH, MLA_KV_RANK)),
        'mla_w_ukv': nrm(ks[15], (DEPTH, MLA_KV_RANK, HC * (MLA_NOPE + MLA_V)), MLA_KV_RANK ** -0.5),
        'gdn_conv_w': nrm(ks[16], (DEPTH, GDN_CONV, HD * (2 * DK_D + DV_D)), GDN_CONV ** -0.5),
        'gdn_a_log': jnp.log(jax.random.uniform(ks[17], (DEPTH, HD), jnp.float32, 1.0, 16.0)),
        'gdn_dt_bias': dt + jnp.log(-jnp.expm1(-dt)),
        'gdn_norm_g': gain(ks[18], (DEPTH, DV_D)),
        'ffn_w_gate': nrm(ks[20], (DEPTH, D_MODEL, D_FF), D_MODEL ** -0.5),
        'ffn_w_up': nrm(ks[21], (DEPTH, D_MODEL, D_FF), D_MODEL ** -0.5),
        'ffn_conv_w': nrm(ks[22], (DEPTH, FFN_CONV, D_FF), FFN_CONV ** -0.5),
        'ffn_w_down': nrm(ks[23], (DEPTH, D_FF, D_MODEL), D_FF ** -0.5),
    }


def reference(x, w_in, w_out, pre_mix_g, post_mix_g, pre_ffn_g, post_ffn_g,
              hgrn_lb_logits, hgrn_norm_g, gla_w_gk2, gla_b_gk, gla_norm_g,
              mla_q_norm_g, mla_w_uq, mla_kv_norm_g, mla_w_ukv,
              gdn_conv_w, gdn_a_log, gdn_dt_bias, gdn_norm_g,
              ffn_w_gate, ffn_w_up, ffn_conv_w, ffn_w_down):
    s = x.shape[1]
    cos, sin = rope_tables(s)
    lower_bounds = hgrn_lower_bounds(hgrn_lb_logits)
    for l in range(DEPTH):
        h = rms_norm(x, pre_mix_g[l])
        z = h @ w_in[l]
        (a_q, a_f, a_i, a_g,
         b_q, b_k, b_v, b_code, b_g,
         c_q, c_kv, c_kr,
         d_q, d_k, d_v, d_beta, d_a, d_z) = split_columns(z)
        o_a = hgrn2_mixer(a_q, a_f, a_i, a_g, lower_bounds[l], hgrn_norm_g[l])
        o_b = gla_mixer(b_q, b_k, b_v, b_code, b_g, gla_w_gk2[l], gla_b_gk[l], gla_norm_g[l])
        o_c = mla_mixer(c_q, c_kv, c_kr, mla_q_norm_g[l], mla_w_uq[l], mla_kv_norm_g[l], mla_w_ukv[l], cos, sin)
        o_d = gdn_mixer(d_q, d_k, d_v, d_beta, d_a, d_z, gdn_conv_w[l], gdn_a_log[l], gdn_dt_bias[l], gdn_norm_g[l])
        mix = jnp.concatenate([o_a, o_b, o_c, o_d], axis=-1).astype(h.dtype) @ w_out[l]
        x = x + rms_norm(mix, post_mix_g[l])
        h = rms_norm(x, pre_ffn_g[l])
        y = conv_glu_ffn(h, ffn_w_gate[l], ffn_w_up[l], ffn_conv_w[l], ffn_w_down[l])
        x = x + rms_norm(y, post_ffn_g[l])
    return x
```

```python
import functools

import jax
import jax.numpy as jnp
from jax import lax
from jax.experimental import pallas as pl
from jax.experimental.pallas import tpu as pltpu

F32 = jnp.float32
BF16 = jnp.bfloat16
HIGHEST = lax.Precision.HIGHEST

N_GROUPS = 4
HEAD_DIM = 64
N_HEADS = 4
GROUP_WIDTH = N_HEADS * HEAD_DIM
EPS = 1e-6
MASK_VALUE = -1e30
MIN_POS = 1e-30

DK_B = HEAD_DIM // 2
GLA_GATE_RANK = 16
GLA_GATE_NORMALIZER = 16.0
MLA_Q_RANK = 256
MLA_KV_RANK = 128
MLA_NOPE = 64
MLA_ROPE = 32
ROPE_THETA = 10000.0
GDN_CONV = 4
FFN_CONV = 3
LANE = 128
SUBLANE = 8

M_CODE = 0
M_BETA = 16
M_A = 20
M_KR = 64

BAND = 16
GDN_CHUNK = 64
VMEM_LIMIT = 56 * 1024 * 1024


def _cparams(sem):
    return pltpu.CompilerParams(dimension_semantics=sem, vmem_limit_bytes=VMEM_LIMIT)


def _rms(x, g):
    return x * lax.rsqrt(jnp.mean(x * x, axis=-1, keepdims=True) + EPS) * g


def _sigmoid(x):
    return jax.nn.sigmoid(x)


def _softplus(x):
    return jnp.maximum(x, 0.0) + jnp.log1p(jnp.exp(-jnp.abs(x)))


def _block_cumsum(x, rowmod, block):
    s = 1
    while s < block:
        x = x + jnp.where(rowmod >= s, pltpu.roll(x, s, 0), 0.0)
        s *= 2
    return x


def _dot(a, b, dims=(((1,), (0,)), ((), ())), precision=None):
    return lax.dot_general(a, b, dims, precision=precision, preferred_element_type=F32)


NT = (((1,), (1,)), ((), ()))
TN = (((0,), (0,)), ((), ()))


def _in_proj_kernel(x_ref, g_ref, w_ref, *out_refs):
    h = _rms(x_ref[...], g_ref[...]).astype(BF16)
    off = 0
    for o_ref in out_refs:
        n = o_ref.shape[-1]
        o_ref[...] = _dot(h, w_ref[:, off:off + n])
        off += n


def _in_proj(x2, g, w, widths, tm):
    t, d = x2.shape
    n = w.shape[1]
    return pl.pallas_call(
        _in_proj_kernel,
        grid=(t // tm,),
        in_specs=[pl.BlockSpec((tm, d), lambda i: (i, 0)),
                  pl.BlockSpec((1, d), lambda i: (0, 0)),
                  pl.BlockSpec((d, n), lambda i: (0, 0))],
        out_specs=[pl.BlockSpec((tm, wd), lambda i: (i, 0)) for wd in widths],
        out_shape=[jax.ShapeDtypeStruct((t, wd), F32) for wd in widths],
        compiler_params=_cparams(("parallel",)),
        name="in_proj",
    )(x2, g, w)


def _gla_core(q, k, v, lg, s_ref, oacc_ref, dkh):
    tt, dk = q.shape
    dv = v.shape[1]
    rowmod_k = lax.broadcasted_iota(jnp.int32, (tt, dk), 0) & (BAND - 1)
    rowmod_v = lax.broadcasted_iota(jnp.int32, (tt, dv), 0) & (BAND - 1)
    b = _block_cumsum(lg, rowmod_k, BAND)

    seg = (lax.broadcasted_iota(jnp.int32, (dk, dv), 0) // dkh
           == lax.broadcasted_iota(jnp.int32, (dk, dv), 1) // HEAD_DIM).astype(BF16)
    o = _dot((q * k).astype(BF16), seg) * v
    for d in range(1, BAND):
        kd = pltpu.roll(k, d, 0)
        bd = pltpu.roll(b, d, 0)
        vd = pltpu.roll(v, d, 0)
        p = q * kd * jnp.exp(jnp.minimum(b - bd, 0.0))
        w = _dot(p.astype(BF16), seg)
        o = o + jnp.where(rowmod_v >= d, w * vd, 0.0)

    head_mask = (lax.broadcasted_iota(jnp.int32, (dv, dk), 0) // HEAD_DIM
                 == lax.broadcasted_iota(jnp.int32, (dv, dk), 1) // dkh)
    for n in range(tt // BAND):
        r0 = n * BAND
        bn = b[r0:r0 + BAND]
        b_last = bn[BAND - 1:BAND]
        qh = (q[r0:r0 + BAND] * jnp.exp(bn)).astype(BF16)
        kt = (k[r0:r0 + BAND] * jnp.exp(b_last - bn)).astype(BF16)
        st = s_ref[...]
        oacc_ref[r0:r0 + BAND, :] = _dot(qh, jnp.where(head_mask, st, 0.0).astype(BF16), NT)
        s_ref[...] = st * jnp.exp(b_last) + _dot(v[r0:r0 + BAND].astype(BF16), kt, TN)
    return o + oacc_ref[...]


def _head_norm_gate(o, ng, gate_act, o_ref):
    for h in range(N_HEADS):
        sl = slice(h * HEAD_DIM, (h + 1) * HEAD_DIM)
        o_ref[0, :, sl] = _rms(o[:, sl], ng) * gate_act[:, sl]


def _hgrn_kernel(z_ref, lb_ref, ng_ref, o_ref, s_ref, oacc_ref):
    @pl.when(pl.program_id(1) == 0)
    def _():
        s_ref[...] = jnp.zeros_like(s_ref)

    gw = GROUP_WIDTH
    q = z_ref[0, :, 0:gw]
    zf = z_ref[0, :, gw:2 * gw]
    v = z_ref[0, :, 2 * gw:3 * gw]
    gate = z_ref[0, :, 3 * gw:4 * gw]
    lb = lb_ref[...]
    f = lb + (1.0 - lb) * _sigmoid(zf)
    lg = jnp.log(jnp.maximum(f, MIN_POS))
    k = (1.0 - lb) * _sigmoid(-zf)
    qh = q * _sigmoid(q) * HEAD_DIM ** -0.5
    o = _gla_core(qh, k, v, lg, s_ref, oacc_ref, HEAD_DIM)
    _head_norm_gate(o, ng_ref[...], _sigmoid(gate), o_ref)


def _gla_kernel(z_ref, m_ref, wgk_ref, bgk_ref, ng_ref, o_ref, s_ref, oacc_ref):
    @pl.when(pl.program_id(1) == 0)
    def _():
        s_ref[...] = jnp.zeros_like(s_ref)

    dk = N_HEADS * DK_B
    q = z_ref[0, :, 0:dk]
    k = z_ref[0, :, dk:2 * dk]
    v = z_ref[0, :, 2 * dk:2 * dk + GROUP_WIDTH]
    gate = z_ref[0, :, 2 * dk + GROUP_WIDTH:2 * dk + 2 * GROUP_WIDTH]
    x = _dot(m_ref[0].astype(BF16), wgk_ref[...]) + bgk_ref[...]
    lg = -_softplus(-x) / GLA_GATE_NORMALIZER
    o = _gla_core(q * DK_B ** -0.5, k, v, lg, s_ref, oacc_ref, DK_B)
    _head_norm_gate(o, ng_ref[...], gate * _sigmoid(gate), o_ref)


def _row(a):
    return pl.BlockSpec(a.shape, lambda b, t: (0,) * a.ndim)


def _hgrn(za, lb, ng, tt):
    bsz, s, _ = za.shape
    return pl.pallas_call(
        _hgrn_kernel,
        grid=(bsz, s // tt),
        in_specs=[pl.BlockSpec((1, tt, za.shape[2]), lambda b, t: (b, t, 0)), _row(lb), _row(ng)],
        out_specs=pl.BlockSpec((1, tt, GROUP_WIDTH), lambda b, t: (b, t, 0)),
        out_shape=jax.ShapeDtypeStruct((bsz, s, GROUP_WIDTH), F32),
        scratch_shapes=[pltpu.VMEM((GROUP_WIDTH, GROUP_WIDTH), F32), pltpu.VMEM((tt, GROUP_WIDTH), F32)],
        compiler_params=_cparams(("parallel", "arbitrary")),
        name="hgrn",
    )(za, lb, ng)


def _gla(zb, zm, wgk, bgk, ng, tt):
    bsz, s, _ = zb.shape
    return pl.pallas_call(
        _gla_kernel,
        grid=(bsz, s // tt),
        in_specs=[pl.BlockSpec((1, tt, zb.shape[2]), lambda b, t: (b, t, 0)),
                  pl.BlockSpec((1, tt, LANE), lambda b, t: (b, t, 0)),
                  _row(wgk), _row(bgk), _row(ng)],
        out_specs=pl.BlockSpec((1, tt, GROUP_WIDTH), lambda b, t: (b, t, 0)),
        out_shape=jax.ShapeDtypeStruct((bsz, s, GROUP_WIDTH), F32),
        scratch_shapes=[pltpu.VMEM((GROUP_WIDTH, N_HEADS * DK_B), F32), pltpu.VMEM((tt, GROUP_WIDTH), F32)],
        compiler_params=_cparams(("parallel", "arbitrary")),
        name="gla",
    )(zb, zm, wgk, bgk, ng)


def _gdn_kernel(z_ref, m_ref, cw_ref, apad_ref, dtpad_ref, ebeta_ref, ea_ref, ng_ref,
                o_ref, xs_ref, s_ref, oacc_ref):
    tt = z_ref.shape[1]
    gw = GROUP_WIDTH
    c = GDN_CHUNK
    halo = SUBLANE

    @pl.when(pl.program_id(1) == 0)
    def _():
        s_ref[...] = jnp.zeros_like(s_ref)
        xs_ref[0:halo, :] = jnp.zeros((halo, 3 * gw), F32)

    xs_ref[halo:halo + tt, :] = z_ref[0, :, 0:3 * gw]
    acc = cw_ref[GDN_CONV - 1:GDN_CONV, :] * xs_ref[halo:halo + tt, :]
    for j in range(1, GDN_CONV):
        acc = acc + cw_ref[GDN_CONV - 1 - j:GDN_CONV - j, :] * xs_ref[halo - j:halo - j + tt, :]
    xs_ref[0:halo, :] = xs_ref[tt:tt + halo, :]
    qkv = acc * _sigmoid(acc)

    m = m_ref[0]
    beta_x = _dot(_sigmoid(m), ebeta_ref[...], precision=HIGHEST)
    lg_m = -apad_ref[...] * _softplus(m + dtpad_ref[...])
    g_x = _dot(lg_m, ea_ref[...], precision=HIGHEST)
    rowmod = lax.broadcasted_iota(jnp.int32, (tt, gw), 0) & (c - 1)
    b_x = _block_cumsum(g_x, rowmod, c)

    ii = lax.broadcasted_iota(jnp.int32, (c, c), 0)
    jj = lax.broadcasted_iota(jnp.int32, (c, c), 1)
    incl = ii >= jj
    strict = ii > jj
    tril = incl.astype(F32)
    upper = (ii > jj).astype(F32)

    for h in range(N_HEADS):
        sl = slice(h * HEAD_DIM, (h + 1) * HEAD_DIM)
        qh_all = qkv[:, h * HEAD_DIM:(h + 1) * HEAD_DIM]
        kh_all = qkv[:, gw + h * HEAD_DIM:gw + (h + 1) * HEAD_DIM]
        vh_all = qkv[:, 2 * gw + h * HEAD_DIM:2 * gw + (h + 1) * HEAD_DIM]
        qn_all = qh_all * lax.rsqrt(jnp.sum(qh_all * qh_all, axis=-1, keepdims=True) + EPS) * HEAD_DIM ** -0.5
        kn_all = kh_all * lax.rsqrt(jnp.sum(kh_all * kh_all, axis=-1, keepdims=True) + EPS)
        for n in range(tt // c):
            rs = slice(n * c, (n + 1) * c)
            qh, kh, vh = qn_all[rs], kn_all[rs], vh_all[rs]
            beta = beta_x[rs, sl]
            g = g_x[rs, sl]
            bcol = b_x[rs, sl]
            diff = _dot(tril, g * upper, precision=HIGHEST)
            lmask = jnp.where(incl, jnp.exp(jnp.minimum(diff, 0.0)), 0.0)
            kb = kh * beta
            kh16 = kh.astype(BF16)
            nm = jnp.where(strict, -_dot(kb.astype(BF16), kh16, NT) * lmask, 0.0)
            tn = nm
            pw = nm
            for _ in range(5):
                pw16 = pw.astype(BF16)
                pw = _dot(pw16, pw16)
                tn = tn + pw + _dot(tn.astype(BF16), pw.astype(BF16))
            tn16 = tn.astype(BF16)
            vb = vh * beta
            kbe = kb * jnp.exp(bcol)
            u = vb + _dot(tn16, vb.astype(BF16))
            w = kbe + _dot(tn16, kbe.astype(BF16))
            a_qk = _dot(qh.astype(BF16), kh16, NT) * lmask
            st = s_ref[h]
            st16 = st.astype(BF16)
            v_new = u - _dot(w.astype(BF16), st16)
            v_new16 = v_new.astype(BF16)
            oacc_ref[rs, sl] = (_dot((qh * jnp.exp(bcol)).astype(BF16), st16)
                                + _dot(a_qk.astype(BF16), v_new16))
            b_last = bcol[c - 1:c, :]
            s_ref[h] = st * jnp.exp(b_last) + _dot((kh * jnp.exp(b_last - bcol)).astype(BF16), v_new16, TN)

    zg = z_ref[0, :, 3 * gw:4 * gw]
    _head_norm_gate(oacc_ref[...], ng_ref[...], zg * _sigmoid(zg), o_ref)


def _gdn(zd, zm, cw, apad, dtpad, ebeta, ea, ng, tt):
    bsz, s, _ = zd.shape
    return pl.pallas_call(
        _gdn_kernel,
        grid=(bsz, s // tt),
        in_specs=[pl.BlockSpec((1, tt, zd.shape[2]), lambda b, t: (b, t, 0)),
                  pl.BlockSpec((1, tt, LANE), lambda b, t: (b, t, 0)),
                  _row(cw), _row(apad), _row(dtpad), _row(ebeta), _row(ea), _row(ng)],
        out_specs=pl.BlockSpec((1, tt, GROUP_WIDTH), lambda b, t: (b, t, 0)),
        out_shape=jax.ShapeDtypeStruct((bsz, s, GROUP_WIDTH), F32),
        scratch_shapes=[pltpu.VMEM((tt + 2 * SUBLANE, 3 * GROUP_WIDTH), F32),
                        pltpu.VMEM((N_HEADS, HEAD_DIM, HEAD_DIM), F32),
                        pltpu.VMEM((tt, GROUP_WIDTH), F32)],
        compiler_params=_cparams(("parallel", "arbitrary")),
        name="gdn",
    )(zd, zm, cw, apad, dtpad, ebeta, ea, ng)


def _mla_pre_kernel(zc_ref, m1_ref, m2_ref, gq_ref, gkv_ref, wqa_ref, wqb_ref, wk_ref, wv_ref,
                    cosq_ref, sinq_ref, cosk_ref, q_ref, k_ref, v_ref):
    scale = (MLA_NOPE + MLA_ROPE) ** -0.5
    cq = _rms(zc_ref[0, :, 0:MLA_Q_RANK], gq_ref[...]).astype(BF16)
    ckv = _rms(zc_ref[0, :, MLA_Q_RANK:MLA_Q_RANK + MLA_KV_RANK], gkv_ref[...]).astype(BF16)
    qa = _dot(cq, wqa_ref[...])
    qb = _dot(cq, wqb_ref[...])
    kn = _dot(ckv, wk_ref[...])
    vv = _dot(ckv, wv_ref[...])
    cosq, sinq = cosq_ref[...], sinq_ref[...]
    kr = m1_ref[0] * cosk_ref[...] + m2_ref[0] * sinq
    for h in range(N_HEADS):
        sl = slice(h * LANE, (h + 1) * LANE)
        q_ref[0, h] = ((qa[:, sl] * cosq + qb[:, sl] * sinq) * scale).astype(BF16)
        k_ref[0, h] = (kn[:, sl] + kr).astype(BF16)
        v_ref[0, h] = vv[:, h * HEAD_DIM:(h + 1) * HEAD_DIM].astype(BF16)


def _mla_pre(zc, zm, zm2, gq, gkv, wqa, wqb, wk, wv, cosq, sinq, cosk, tm):
    bsz, s, _ = zc.shape
    tok = lambda wd: pl.BlockSpec((1, tm, wd), lambda b, t: (b, t, 0))
    tab = pl.BlockSpec((tm, LANE), lambda b, t: (t, 0))
    hm = lambda wd: pl.BlockSpec((1, N_HEADS, tm, wd), lambda b, t: (b, 0, t, 0))
    return pl.pallas_call(
        _mla_pre_kernel,
        grid=(bsz, s // tm),
        in_specs=[tok(zc.shape[2]), tok(LANE), tok(LANE), _row(gq), _row(gkv),
                  _row(wqa), _row(wqb), _row(wk), _row(wv), tab, tab, tab],
        out_specs=[hm(LANE), hm(LANE), hm(HEAD_DIM)],
        out_shape=[jax.ShapeDtypeStruct((bsz, N_HEADS, s, LANE), BF16),
                   jax.ShapeDtypeStruct((bsz, N_HEADS, s, LANE), BF16),
                   jax.ShapeDtypeStruct((bsz, N_HEADS, s, HEAD_DIM), BF16)],
        compiler_params=_cparams(("parallel", "parallel")),
        name="mla_pre",
    )(zc, zm, zm2, gq, gkv, wqa, wqb, wk, wv, cosq, sinq, cosk)


def _mla_attn_kernel(q_ref, k_ref, v_ref, o_ref, m_ref, l_ref, acc_ref):
    i = pl.program_id(1)
    j = pl.program_id(2)
    tq = q_ref.shape[2]
    tk = k_ref.shape[2]

    @pl.when(j == 0)
    def _():
        m_ref[...] = jnp.full_like(m_ref, MASK_VALUE)
        l_ref[...] = jnp.zeros_like(l_ref)
        acc_ref[...] = jnp.zeros_like(acc_ref)

    def step(masked):
        for h in range(N_HEADS):
            s = _dot(q_ref[0, h], k_ref[0, h], NT)
            if masked:
                keep = (lax.broadcasted_iota(jnp.int32, (tq, tk), 1)
                        <= lax.broadcasted_iota(jnp.int32, (tq, tk), 0))
                s = jnp.where(keep, s, MASK_VALUE)
            m_prev = m_ref[h]
            m_new = jnp.maximum(m_prev, jnp.max(s, axis=-1, keepdims=True))
            alpha = jnp.exp(m_prev - m_new)
            p = jnp.exp(s - m_new)
            l_ref[h] = alpha * l_ref[h] + jnp.sum(p, axis=-1, keepdims=True)
            acc_ref[h] = alpha * acc_ref[h] + _dot(p.astype(BF16), v_ref[0, h])
            m_ref[h] = m_new

    @pl.when(j < i)
    def _():
        step(False)

    @pl.when(j == i)
    def _():
        step(True)
        for h in range(N_HEADS):
            o_ref[0, :, h * HEAD_DIM:(h + 1) * HEAD_DIM] = acc_ref[h] / l_ref[h]


def _mla_attn(q, k, v, tq):
    bsz, _, s, _ = q.shape
    nq = s // tq
    kv_map = lambda b, i, j: (b, 0, jnp.minimum(i, j), 0)
    return pl.pallas_call(
        _mla_attn_kernel,
        grid=(bsz, nq, nq),
        in_specs=[pl.BlockSpec((1, N_HEADS, tq, LANE), lambda b, i, j: (b, 0, i, 0)),
                  pl.BlockSpec((1, N_HEADS, tq, LANE), kv_map),
                  pl.BlockSpec((1, N_HEADS, tq, HEAD_DIM), kv_map)],
        out_specs=pl.BlockSpec((1, tq, GROUP_WIDTH), lambda b, i, j: (b, i, 0)),
        out_shape=jax.ShapeDtypeStruct((bsz, s, GROUP_WIDTH), F32),
        scratch_shapes=[pltpu.VMEM((N_HEADS, tq, 1), F32), pltpu.VMEM((N_HEADS, tq, 1), F32),
                        pltpu.VMEM((N_HEADS, tq, HEAD_DIM), F32)],
        compiler_params=_cparams(("parallel", "parallel", "arbitrary")),
        name="mla_attn",
    )(q, k, v)


def _out_proj_kernel(oa_ref, ob_ref, oc_ref, od_ref, w_ref, g_ref, x_ref, y_ref):
    gw = GROUP_WIDTH
    mix = _dot(oa_ref[...].astype(BF16), w_ref[0:gw, :])
    for n, r in enumerate((ob_ref, oc_ref, od_ref), start=1):
        mix = mix + _dot(r[...].astype(BF16), w_ref[n * gw:(n + 1) * gw, :])
    y_ref[...] = x_ref[...] + _rms(mix, g_ref[...])


def _out_proj(outs, w, g, x2, tm):
    t, d = x2.shape
    grp = pl.BlockSpec((tm, GROUP_WIDTH), lambda i: (i, 0))
    return pl.pallas_call(
        _out_proj_kernel,
        grid=(t // tm,),
        in_specs=[grp, grp, grp, grp,
                  pl.BlockSpec(w.shape, lambda i: (0, 0)),
                  pl.BlockSpec((1, d), lambda i: (0, 0)),
                  pl.BlockSpec((tm, d), lambda i: (i, 0))],
        out_specs=pl.BlockSpec((tm, d), lambda i: (i, 0)),
        out_shape=jax.ShapeDtypeStruct((t, d), F32),
        compiler_params=_cparams(("parallel",)),
        name="out_proj",
    )(*outs, w, g, x2)


def _gelu_tanh(x):
    return 0.5 * x * (1.0 + jnp.tanh(0.7978845608028654 * (x + 0.044715 * (x * x * x))))


def _ffn_kernel(x_ref, g1_ref, wg_ref, wu_ref, cw_ref, wd_ref, g2_ref, y_ref, prev_ref, *, fb):
    tm = x_ref.shape[1]
    f = wg_ref.shape[1]

    @pl.when(pl.program_id(1) == 0)
    def _():
        prev_ref[...] = jnp.zeros_like(prev_ref)

    x = x_ref[0]
    h = _rms(x, g1_ref[...]).astype(BF16)
    row = lax.broadcasted_iota(jnp.int32, (tm, fb), 0)
    acc = jnp.zeros((tm, x.shape[1]), F32)
    for n in range(f // fb):
        cs = slice(n * fb, (n + 1) * fb)
        gate = _dot(h, wg_ref[:, cs])
        up = _dot(h, wu_ref[:, cs])
        p1 = prev_ref[1:2, cs]
        p2 = prev_ref[0:1, cs]
        g1 = jnp.where(row == 0, p1, pltpu.roll(gate, 1, 0))
        g2 = jnp.where(row == 0, p2, jnp.where(row == 1, p1, pltpu.roll(gate, 2, 0)))
        prev_ref[0:2, cs] = gate[tm - 2:tm, :]
        conv = cw_ref[2:3, cs] * gate + cw_ref[1:2, cs] * g1 + cw_ref[0:1, cs] * g2
        acc = acc + _dot((_gelu_tanh(conv) * up).astype(BF16), wd_ref[cs, :])
    y_ref[0] = x + _rms(acc, g2_ref[...])


def _ffn(x, g1, wg, wu, cw, wd, g2, tm, fb):
    bsz, s, d = x.shape
    return pl.pallas_call(
        functools.partial(_ffn_kernel, fb=fb),
        grid=(bsz, s // tm),
        in_specs=[pl.BlockSpec((1, tm, d), lambda b, t: (b, t, 0)),
                  _row(g1), _row(wg), _row(wu), _row(cw), _row(wd), _row(g2)],
        out_specs=pl.BlockSpec((1, tm, d), lambda b, t: (b, t, 0)),
        out_shape=jax.ShapeDtypeStruct((bsz, s, d), F32),
        scratch_shapes=[pltpu.VMEM((SUBLANE, wg.shape[1]), F32)],
        compiler_params=_cparams(("parallel", "arbitrary")),
        name="ffn",
    )(x, g1, wg, wu, cw, wd, g2)


def _place(width, pieces):
    rows = pieces[0][1].shape[0]
    out = jnp.zeros((rows, width), pieces[0][1].dtype)
    for off, a in pieces:
        out = lax.dynamic_update_slice(out, a, (0, off))
    return out


def _rot_cols(w):
    half = w.shape[1] // 2
    return jnp.concatenate([-w[:, half:], w[:, :half]], axis=1)


def _in_proj_layout(w):
    gw, dkb = GROUP_WIDTH, N_HEADS * DK_B
    o_b = 4 * gw
    o_code = o_b + 2 * dkb + gw
    o_bg = o_code + GLA_GATE_RANK
    o_c = o_bg + gw
    o_kr = o_c + MLA_Q_RANK + MLA_KV_RANK
    o_d = o_kr + MLA_ROPE
    o_beta = o_d + 3 * gw
    o_a = o_beta + N_HEADS
    o_z = o_a + N_HEADS
    assert o_z + gw == w.shape[1]
    kr = w[:, o_kr:o_d]
    misc1 = _place(LANE, [(M_CODE, w[:, o_code:o_bg]), (M_BETA, w[:, o_beta:o_a]),
                          (M_A, w[:, o_a:o_z]), (M_KR, kr)])
    misc2 = _place(LANE, [(M_KR, _rot_cols(kr))])
    cols = [w[:, 0:o_b],
            w[:, o_b:o_code], w[:, o_bg:o_c],
            w[:, o_c:o_kr],
            w[:, o_d:o_beta], w[:, o_z:],
            misc1, misc2]
    widths = (4 * gw, 2 * dkb + 2 * gw, MLA_Q_RANK + MLA_KV_RANK, 4 * gw, LANE, LANE)
    return jnp.concatenate(cols, axis=1).astype(BF16), widths


def _mla_layout(w_uq, w_ukv):
    dq = MLA_NOPE + MLA_ROPE
    qa, qb, wk, wv = [], [], [], []
    for h in range(N_HEADS):
        nope = w_uq[:, h * dq:h * dq + MLA_NOPE]
        rope = w_uq[:, h * dq + MLA_NOPE:(h + 1) * dq]
        qa.append(_place(LANE, [(0, nope), (MLA_NOPE, rope)]))
        qb.append(_place(LANE, [(MLA_NOPE, _rot_cols(rope))]))
        kv = w_ukv[:, h * 2 * HEAD_DIM:(h + 1) * 2 * HEAD_DIM]
        wk.append(_place(LANE, [(0, kv[:, :MLA_NOPE])]))
        wv.append(kv[:, MLA_NOPE:])
    cat = lambda xs: jnp.concatenate(xs, axis=1).astype(BF16)
    return cat(qa), cat(qb), cat(wk), cat(wv)


def _rope_tables(s):
    inv = ROPE_THETA ** (-jnp.arange(0, MLA_ROPE, 2, dtype=F32) / MLA_ROPE)
    ang = jnp.arange(s, dtype=F32)[:, None] * inv[None, :]
    cos2 = jnp.concatenate([jnp.cos(ang), jnp.cos(ang)], axis=1)
    sin2 = jnp.concatenate([jnp.sin(ang), jnp.sin(ang)], axis=1)
    ones = jnp.ones((s, MLA_NOPE), F32)
    cosq = _place(LANE, [(0, ones), (MLA_NOPE, cos2)])
    sinq = _place(LANE, [(MLA_NOPE, sin2)])
    cosk = _place(LANE, [(MLA_NOPE, cos2)])
    return cosq, sinq, cosk


def _tile_heads(g):
    return jnp.tile(g, N_HEADS)[None, :]


def _pick(n, prefs):
    for p in prefs:
        if n % p == 0:
            return p
    return n


def kernel(x, w_in, w_out, pre_mix_g, post_mix_g, pre_ffn_g, post_ffn_g, hgrn_lb_logits, hgrn_norm_g,
           gla_w_gk2, gla_b_gk, gla_norm_g, mla_q_norm_g, mla_w_uq, mla_kv_norm_g, mla_w_ukv,
           gdn_conv_w, gdn_a_log, gdn_dt_bias, gdn_norm_g, ffn_w_gate, ffn_w_up, ffn_conv_w, ffn_w_down):
    bsz, s, d = x.shape
    depth = w_in.shape[0]
    t = bsz * s
    tm = _pick(t, (256,))
    tt = _pick(s, (256,))
    tq = _pick(s, (512, 256, 128))
    fb = 256

    cosq, sinq, cosk = _rope_tables(s)
    p = jax.nn.softmax(hgrn_lb_logits.astype(F32), axis=0)
    lower_bounds = jnp.cumsum(p, axis=0) - p[0]

    lanes = jnp.arange(LANE)
    head_of = jnp.arange(GROUP_WIDTH) // HEAD_DIM
    ebeta = (lanes[:, None] == M_BETA + head_of[None, :]).astype(F32)
    ea = (lanes[:, None] == M_A + head_of[None, :]).astype(F32)

    for l in range(depth):
        w_in_l, widths = _in_proj_layout(w_in[l])
        za, zb, zc, zd, zm, zm2 = _in_proj(x.reshape(t, d), pre_mix_g[l][None, :], w_in_l, widths, tm)
        r3 = lambda a: a.reshape(bsz, s, a.shape[-1])
        za, zb, zc, zd, zm, zm2 = map(r3, (za, zb, zc, zd, zm, zm2))

        o_a = _hgrn(za, lower_bounds[l][None, :], hgrn_norm_g[l][None, :], tt)

        wgk = _place(N_HEADS * DK_B, [(0, gla_w_gk2[l])])
        wgk = jnp.concatenate([wgk, jnp.zeros((LANE - GLA_GATE_RANK, N_HEADS * DK_B), F32)], axis=0).astype(BF16)
        o_b = _gla(zb, zm, wgk, gla_b_gk[l][None, :], gla_norm_g[l][None, :], tt)

        wqa, wqb, wk, wv = _mla_layout(mla_w_uq[l], mla_w_ukv[l])
        qh, kh, vh = _mla_pre(zc, zm, zm2, mla_q_norm_g[l][None, :], mla_kv_norm_g[l][None, :],
                              wqa, wqb, wk, wv, cosq, sinq, cosk, tt)
        o_c = _mla_attn(qh, kh, vh, tq)

        apad = _place(LANE, [(M_A, jnp.exp(gdn_a_log[l].astype(F32))[None, :])])
        dtpad = _place(LANE, [(M_A, gdn_dt_bias[l][None, :])])
        o_d = _gdn(zd, zm, gdn_conv_w[l], apad, dtpad, ebeta, ea, gdn_norm_g[l][None, :], tt)

        outs = [o.reshape(t, GROUP_WIDTH) for o in (o_a, o_b, o_c, o_d)]
        x = _out_proj(outs, w_out[l].astype(BF16), post_mix_g[l][None, :], x.reshape(t, d), tm).reshape(bsz, s, d)

        x = _ffn(x, pre_ffn_g[l][None, :], ffn_w_gate[l].astype(BF16), ffn_w_up[l].astype(BF16),
                 ffn_conv_w[l], ffn_w_down[l].astype(BF16), post_ffn_g[l][None, :], tm=_pick(s, (256,)), fb=fb)
    return x
```

```python
import functools

import jax
import jax.numpy as jnp
from jax import lax
from jax.experimental import pallas as pl
from jax.experimental.pallas import tpu as pltpu

F32 = jnp.float32
BF16 = jnp.bfloat16
HIGHEST = lax.Precision.HIGHEST

N_GROUPS = 4
HEAD_DIM = 64
N_HEADS = 4
GROUP_WIDTH = N_HEADS * HEAD_DIM
EPS = 1e-6
MASK_VALUE = -1e30
MIN_POS = 1e-30

DK_B = HEAD_DIM // 2
GLA_GATE_RANK = 16
GLA_GATE_NORMALIZER = 16.0
MLA_Q_RANK = 256
MLA_KV_RANK = 128
MLA_NOPE = 64
MLA_ROPE = 32
ROPE_THETA = 10000.0
GDN_CONV = 4
FFN_CONV = 3
LANE = 128
SUBLANE = 8

M_CODE = 0
M_BETA = 16
M_A = 20
M_KR = 64

BAND = 16
GDN_CHUNK = 64
VMEM_LIMIT = 56 * 1024 * 1024


def _cparams(sem):
    return pltpu.CompilerParams(dimension_semantics=sem, vmem_limit_bytes=VMEM_LIMIT)


def _rms(x, g):
    return x * lax.rsqrt(jnp.mean(x * x, axis=-1, keepdims=True) + EPS) * g


def _sigmoid(x):
    return jax.nn.sigmoid(x)


def _softplus(x):
    return jnp.maximum(x, 0.0) + jnp.log1p(jnp.exp(-jnp.abs(x)))


def _block_cumsum(x, rowmod, block):
    s = 1
    while s < block:
        x = x + jnp.where(rowmod >= s, pltpu.roll(x, s, 0), 0.0)
        s *= 2
    return x


def _dot(a, b, dims=(((1,), (0,)), ((), ())), precision=None):
    return lax.dot_general(a, b, dims, precision=precision, preferred_element_type=F32)


NT = (((1,), (1,)), ((), ()))
TN = (((0,), (0,)), ((), ()))


def _in_proj_kernel(x_ref, g_ref, w_ref, *out_refs):
    h = _rms(x_ref[...], g_ref[...]).astype(BF16)
    off = 0
    for o_ref in out_refs:
        n = o_ref.shape[-1]
        o_ref[...] = _dot(h, w_ref[:, off:off + n])
        off += n


def _in_proj(x2, g, w, widths, tm):
    t, d = x2.shape
    n = w.shape[1]
    return pl.pallas_call(
        _in_proj_kernel,
        grid=(t // tm,),
        in_specs=[pl.BlockSpec((tm, d), lambda i: (i, 0)),
                  pl.BlockSpec((1, d), lambda i: (0, 0)),
                  pl.BlockSpec((d, n), lambda i: (0, 0))],
        out_specs=[pl.BlockSpec((tm, wd), lambda i: (i, 0)) for wd in widths],
        out_shape=[jax.ShapeDtypeStruct((t, wd), F32) for wd in widths],
        compiler_params=_cparams(("parallel",)),
        name="in_proj",
    )(x2, g, w)


def _gla_core(q, k, v, lg, s_ref, oacc_ref, dkh):
    tt, dk = q.shape
    dv = v.shape[1]
    rowmod_k = lax.broadcasted_iota(jnp.int32, (tt, dk), 0) & (BAND - 1)
    rowmod_v = lax.broadcasted_iota(jnp.int32, (tt, dv), 0) & (BAND - 1)
    b = _block_cumsum(lg, rowmod_k, BAND)

    seg = (lax.broadcasted_iota(jnp.int32, (dk, dv), 0) // dkh
           == lax.broadcasted_iota(jnp.int32, (dk, dv), 1) // HEAD_DIM).astype(BF16)
    o = _dot((q * k).astype(BF16), seg) * v
    for d in range(1, BAND):
        kd = pltpu.roll(k, d, 0)
        bd = pltpu.roll(b, d, 0)
        vd = pltpu.roll(v, d, 0)
        p = q * kd * jnp.exp(jnp.minimum(b - bd, 0.0))
        w = _dot(p.astype(BF16), seg)
        o = o + jnp.where(rowmod_v >= d, w * vd, 0.0)

    head_mask = (lax.broadcasted_iota(jnp.int32, (dv, dk), 0) // HEAD_DIM
                 == lax.broadcasted_iota(jnp.int32, (dv, dk), 1) // dkh)
    for n in range(tt // BAND):
        r0 = n * BAND
        bn = b[r0:r0 + BAND]
        b_last = bn[BAND - 1:BAND]
        qh = (q[r0:r0 + BAND] * jnp.exp(bn)).astype(BF16)
        kt = (k[r0:r0 + BAND] * jnp.exp(b_last - bn)).astype(BF16)
        st = s_ref[...]
        oacc_ref[r0:r0 + BAND, :] = _dot(qh, jnp.where(head_mask, st, 0.0).astype(BF16), NT)
        s_ref[...] = st * jnp.exp(b_last) + _dot(v[r0:r0 + BAND].astype(BF16), kt, TN)
    return o + oacc_ref[...]


def _head_norm_gate(o, ng, gate_act, o_ref):
    for h in range(N_HEADS):
        sl = slice(h * HEAD_DIM, (h + 1) * HEAD_DIM)
        o_ref[0, :, sl] = _rms(o[:, sl], ng) * gate_act[:, sl]


def _hgrn_kernel(z_ref, lb_ref, ng_ref, o_ref, s_ref, oacc_ref):
    @pl.when(pl.program_id(1) == 0)
    def _():
        s_ref[...] = jnp.zeros_like(s_ref)

    gw = GROUP_WIDTH
    q = z_ref[0, :, 0:gw]
    zf = z_ref[0, :, gw:2 * gw]
    v = z_ref[0, :, 2 * gw:3 * gw]
    gate = z_ref[0, :, 3 * gw:4 * gw]
    lb = lb_ref[...]
    f = lb + (1.0 - lb) * _sigmoid(zf)
    lg = jnp.log(jnp.maximum(f, MIN_POS))
    k = (1.0 - lb) * _sigmoid(-zf)
    qh = q * _sigmoid(q) * HEAD_DIM ** -0.5
    o = _gla_core(qh, k, v, lg, s_ref, oacc_ref, HEAD_DIM)
    _head_norm_gate(o, ng_ref[...], _sigmoid(gate), o_ref)


def _gla_kernel(z_ref, m_ref, wgk_ref, bgk_ref, ng_ref, o_ref, s_ref, oacc_ref):
    @pl.when(pl.program_id(1) == 0)
    def _():
        s_ref[...] = jnp.zeros_like(s_ref)

    dk = N_HEADS * DK_B
    q = z_ref[0, :, 0:dk]
    k = z_ref[0, :, dk:2 * dk]
    v = z_ref[0, :, 2 * dk:2 * dk + GROUP_WIDTH]
    gate = z_ref[0, :, 2 * dk + GROUP_WIDTH:2 * dk + 2 * GROUP_WIDTH]
    x = _dot(m_ref[0].astype(BF16), wgk_ref[...]) + bgk_ref[...]
    lg = -_softplus(-x) / GLA_GATE_NORMALIZER
    o = _gla_core(q * DK_B ** -0.5, k, v, lg, s_ref, oacc_ref, DK_B)
    _head_norm_gate(o, ng_ref[...], gate * _sigmoid(gate), o_ref)


def _row(a):
    return pl.BlockSpec(a.shape, lambda b, t: (0,) * a.ndim)


def _hgrn(za, lb, ng, tt):
    bsz, s, _ = za.shape
    return pl.pallas_call(
        _hgrn_kernel,
        grid=(bsz, s // tt),
        in_specs=[pl.BlockSpec((1, tt, za.shape[2]), lambda b, t: (b, t, 0)), _row(lb), _row(ng)],
        out_specs=pl.BlockSpec((1, tt, GROUP_WIDTH), lambda b, t: (b, t, 0)),
        out_shape=jax.ShapeDtypeStruct((bsz, s, GROUP_WIDTH), F32),
        scratch_shapes=[pltpu.VMEM((GROUP_WIDTH, GROUP_WIDTH), F32), pltpu.VMEM((tt, GROUP_WIDTH), F32)],
        compiler_params=_cparams(("parallel", "arbitrary")),
        name="hgrn",
    )(za, lb, ng)


def _gla(zb, zm, wgk, bgk, ng, tt):
    bsz, s, _ = zb.shape
    return pl.pallas_call(
        _gla_kernel,
        grid=(bsz, s // tt),
        in_specs=[pl.BlockSpec((1, tt, zb.shape[2]), lambda b, t: (b, t, 0)),
                  pl.BlockSpec((1, tt, LANE), lambda b, t: (b, t, 0)),
                  _row(wgk), _row(bgk), _row(ng)],
        out_specs=pl.BlockSpec((1, tt, GROUP_WIDTH), lambda b, t: (b, t, 0)),
        out_shape=jax.ShapeDtypeStruct((bsz, s, GROUP_WIDTH), F32),
        scratch_shapes=[pltpu.VMEM((GROUP_WIDTH, N_HEADS * DK_B), F32), pltpu.VMEM((tt, GROUP_WIDTH), F32)],
        compiler_params=_cparams(("parallel", "arbitrary")),
        name="gla",
    )(zb, zm, wgk, bgk, ng)


def _gdn_kernel(z_ref, m_ref, cw_ref, apad_ref, dtpad_ref, ebeta_ref, ea_ref, ng_ref,
                o_ref, xs_ref, s_ref, oacc_ref):
    tt = z_ref.shape[1]
    gw = GROUP_WIDTH
    c = GDN_CHUNK
    halo = SUBLANE

    @pl.when(pl.program_id(1) == 0)
    def _():
        s_ref[...] = jnp.zeros_like(s_ref)
        xs_ref[0:halo, :] = jnp.zeros((halo, 3 * gw), F32)

    xs_ref[halo:halo + tt, :] = z_ref[0, :, 0:3 * gw]
    acc = cw_ref[GDN_CONV - 1:GDN_CONV, :] * xs_ref[halo:halo + tt, :]
    for j in range(1, GDN_CONV):
        acc = acc + cw_ref[GDN_CONV - 1 - j:GDN_CONV - j, :] * xs_ref[halo - j:halo - j + tt, :]
    xs_ref[0:halo, :] = xs_ref[tt:tt + halo, :]
    qkv = acc * _sigmoid(acc)

    m = m_ref[0]
    beta_x = _dot(_sigmoid(m), ebeta_ref[...], precision=HIGHEST)
    lg_m = -apad_ref[...] * _softplus(m + dtpad_ref[...])
    g_x = _dot(lg_m, ea_ref[...], precision=HIGHEST)
    rowmod = lax.broadcasted_iota(jnp.int32, (tt, gw), 0) & (c - 1)
    b_x = _block_cumsum(g_x, rowmod, c)

    ii = lax.broadcasted_iota(jnp.int32, (c, c), 0)
    jj = lax.broadcasted_iota(jnp.int32, (c, c), 1)
    incl = ii >= jj
    strict = ii > jj
    tril = incl.astype(F32)
    upper = (ii > jj).astype(F32)

    for h in range(N_HEADS):
        sl = slice(h * HEAD_DIM, (h + 1) * HEAD_DIM)
        qh_all = qkv[:, h * HEAD_DIM:(h + 1) * HEAD_DIM]
        kh_all = qkv[:, gw + h * HEAD_DIM:gw + (h + 1) * HEAD_DIM]
        vh_all = qkv[:, 2 * gw + h * HEAD_DIM:2 * gw + (h + 1) * HEAD_DIM]
        qn_all = qh_all * lax.rsqrt(jnp.sum(qh_all * qh_all, axis=-1, keepdims=True) + EPS) * HEAD_DIM ** -0.5
        kn_all = kh_all * lax.rsqrt(jnp.sum(kh_all * kh_all, axis=-1, keepdims=True) + EPS)
        for n in range(tt // c):
            rs = slice(n * c, (n + 1) * c)
            qh, kh, vh = qn_all[rs], kn_all[rs], vh_all[rs]
            beta = beta_x[rs, sl]
            g = g_x[rs, sl]
            bcol = b_x[rs, sl]
            diff = _dot(tril, g * upper, precision=HIGHEST)
            lmask = jnp.where(incl, jnp.exp(jnp.minimum(diff, 0.0)), 0.0)
            kb = kh * beta
            kh16 = kh.astype(BF16)
            nm = jnp.where(strict, -_dot(kb.astype(BF16), kh16, NT) * lmask, 0.0)
            tn = nm
            pw = nm
            for _ in range(5):
                pw16 = pw.astype(BF16)
                pw = _dot(pw16, pw16)
                tn = tn + pw + _dot(tn.astype(BF16), pw.astype(BF16))
            tn16 = tn.astype(BF16)
            vb = vh * beta
            kbe = kb * jnp.exp(bcol)
            u = vb + _dot(tn16, vb.astype(BF16))
            w = kbe + _dot(tn16, kbe.astype(BF16))
            a_qk = _dot(qh.astype(BF16), kh16, NT) * lmask
            st = s_ref[h]
            st16 = st.astype(BF16)
            v_new = u - _dot(w.astype(BF16), st16)
            v_new16 = v_new.astype(BF16)
            oacc_ref[rs, sl] = (_dot((qh * jnp.exp(bcol)).astype(BF16), st16)
                                + _dot(a_qk.astype(BF16), v_new16))
            b_last = bcol[c - 1:c, :]
            s_ref[h] = st * jnp.exp(b_last) + _dot((kh * jnp.exp(b_last - bcol)).astype(BF16), v_new16, TN)

    zg = z_ref[0, :, 3 * gw:4 * gw]
    _head_norm_gate(oacc_ref[...], ng_ref[...], zg * _sigmoid(zg), o_ref)


def _gdn(zd, zm, cw, apad, dtpad, ebeta, ea, ng, tt):
    bsz, s, _ = zd.shape
    return pl.pallas_call(
        _gdn_kernel,
        grid=(bsz, s // tt),
        in_specs=[pl.BlockSpec((1, tt, zd.shape[2]), lambda b, t: (b, t, 0)),
                  pl.BlockSpec((1, tt, LANE), lambda b, t: (b, t, 0)),
                  _row(cw), _row(apad), _row(dtpad), _row(ebeta), _row(ea), _row(ng)],
        out_specs=pl.BlockSpec((1, tt, GROUP_WIDTH), lambda b, t: (b, t, 0)),
        out_shape=jax.ShapeDtypeStruct((bsz, s, GROUP_WIDTH), F32),
        scratch_shapes=[pltpu.VMEM((tt + 2 * SUBLANE, 3 * GROUP_WIDTH), F32),
                        pltpu.VMEM((N_HEADS, HEAD_DIM, HEAD_DIM), F32),
                        pltpu.VMEM((tt, GROUP_WIDTH), F32)],
        compiler_params=_cparams(("parallel", "arbitrary")),
        name="gdn",
    )(zd, zm, cw, apad, dtpad, ebeta, ea, ng)


def _mla_pre_kernel(zc_ref, m1_ref, m2_ref, gq_ref, gkv_ref, wqa_ref, wqb_ref, wk_ref, wv_ref,
                    cosq_ref, sinq_ref, cosk_ref, qt_ref, k_ref, vt_ref):
    scale = (MLA_NOPE + MLA_ROPE) ** -0.5
    cq = _rms(zc_ref[0, :, 0:MLA_Q_RANK], gq_ref[...]).astype(BF16)
    ckv = _rms(zc_ref[0, :, MLA_Q_RANK:MLA_Q_RANK + MLA_KV_RANK], gkv_ref[...]).astype(BF16)
    qa = _dot(cq, wqa_ref[...])
    qb = _dot(cq, wqb_ref[...])
    kn = _dot(ckv, wk_ref[...])
    vv = _dot(ckv, wv_ref[...])
    cosq, sinq = cosq_ref[...], sinq_ref[...]
    kr = m1_ref[0] * cosk_ref[...] + m2_ref[0] * sinq
    for h in range(N_HEADS):
        sl = slice(h * LANE, (h + 1) * LANE)
        q = (qa[:, sl] * cosq + qb[:, sl] * sinq) * scale
        qt_ref[0, sl, :] = q.T.astype(BF16)
        k_ref[0, h] = (kn[:, sl] + kr).astype(BF16)
    vt_ref[0] = vv.T.astype(BF16)


def _mla_pre(zc, zm, zm2, gq, gkv, wqa, wqb, wk, wv, cosq, sinq, cosk, tm):
    bsz, s, _ = zc.shape
    tok = lambda wd: pl.BlockSpec((1, tm, wd), lambda b, t: (b, t, 0))
    tab = pl.BlockSpec((tm, LANE), lambda b, t: (t, 0))
    tr = lambda rows: pl.BlockSpec((1, rows, tm), lambda b, t: (b, 0, t))
    return pl.pallas_call(
        _mla_pre_kernel,
        grid=(bsz, s // tm),
        in_specs=[tok(zc.shape[2]), tok(LANE), tok(LANE), _row(gq), _row(gkv),
                  _row(wqa), _row(wqb), _row(wk), _row(wv), tab, tab, tab],
        out_specs=[tr(N_HEADS * LANE),
                   pl.BlockSpec((1, N_HEADS, tm, LANE), lambda b, t: (b, 0, t, 0)),
                   tr(GROUP_WIDTH)],
        out_shape=[jax.ShapeDtypeStruct((bsz, N_HEADS * LANE, s), BF16),
                   jax.ShapeDtypeStruct((bsz, N_HEADS, s, LANE), BF16),
                   jax.ShapeDtypeStruct((bsz, GROUP_WIDTH, s), BF16)],
        compiler_params=_cparams(("parallel", "parallel")),
        name="mla_pre",
    )(zc, zm, zm2, gq, gkv, wqa, wqb, wk, wv, cosq, sinq, cosk)


def _mla_attn_kernel(qi_ref, kj_ref, qt_ref, k_ref, vt_ref, o_ref, m_ref, l_ref, acc_ref):
    pair = pl.program_id(1)
    i = qi_ref[pair]
    j = kj_ref[pair]
    tq = qt_ref.shape[2]
    tk = k_ref.shape[2]

    @pl.when(j == 0)
    def _():
        m_ref[...] = jnp.full_like(m_ref, MASK_VALUE)
        l_ref[...] = jnp.zeros_like(l_ref)
        acc_ref[...] = jnp.zeros_like(acc_ref)

    def step(masked):
        for h in range(N_HEADS):
            s = _dot(k_ref[0, h], qt_ref[0, h * LANE:(h + 1) * LANE, :])
            if masked:
                keep = (lax.broadcasted_iota(jnp.int32, (tk, tq), 0)
                        <= lax.broadcasted_iota(jnp.int32, (tk, tq), 1))
                s = jnp.where(keep, s, MASK_VALUE)
            m_prev = m_ref[h:h + 1, :]
            m_new = jnp.maximum(m_prev, jnp.max(s, axis=0, keepdims=True))
            alpha = jnp.exp(m_prev - m_new)
            p = jnp.exp(s - m_new)
            l_ref[h:h + 1, :] = alpha * l_ref[h:h + 1, :] + jnp.sum(p, axis=0, keepdims=True)
            hs = slice(h * HEAD_DIM, (h + 1) * HEAD_DIM)
            acc_ref[hs, :] = alpha * acc_ref[hs, :] + _dot(vt_ref[0, hs, :], p.astype(BF16))
            m_ref[h:h + 1, :] = m_new

    @pl.when(j < i)
    def _():
        step(False)

    @pl.when(j == i)
    def _():
        step(True)
        for h in range(N_HEADS):
            hs = slice(h * HEAD_DIM, (h + 1) * HEAD_DIM)
            acc_ref[hs, :] = acc_ref[hs, :] / l_ref[h:h + 1, :]
        o_ref[0] = acc_ref[...].T


def _mla_attn(qt, k, vt, tq):
    bsz, _, s = qt.shape
    nq = s // tq
    pairs = [(i, j) for i in range(nq) for j in range(i + 1)]
    qi = jnp.array([p[0] for p in pairs], jnp.int32)
    kj = jnp.array([p[1] for p in pairs], jnp.int32)
    grid_spec = pltpu.PrefetchScalarGridSpec(
        num_scalar_prefetch=2,
        grid=(bsz, len(pairs)),
        in_specs=[pl.BlockSpec((1, N_HEADS * LANE, tq), lambda b, p, qi, kj: (b, 0, qi[p])),
                  pl.BlockSpec((1, N_HEADS, tq, LANE), lambda b, p, qi, kj: (b, 0, kj[p], 0)),
                  pl.BlockSpec((1, GROUP_WIDTH, tq), lambda b, p, qi, kj: (b, 0, kj[p]))],
        out_specs=pl.BlockSpec((1, tq, GROUP_WIDTH), lambda b, p, qi, kj: (b, qi[p], 0)),
        scratch_shapes=[pltpu.VMEM((SUBLANE, tq), F32), pltpu.VMEM((SUBLANE, tq), F32),
                        pltpu.VMEM((GROUP_WIDTH, tq), F32)],
    )
    return pl.pallas_call(
        _mla_attn_kernel,
        grid_spec=grid_spec,
        out_shape=jax.ShapeDtypeStruct((bsz, s, GROUP_WIDTH), F32),
        compiler_params=_cparams(("parallel", "arbitrary")),
        name="mla_attn",
    )(qi, kj, qt, k, vt)


def _out_proj_kernel(oa_ref, ob_ref, oc_ref, od_ref, w_ref, g_ref, x_ref, y_ref):
    gw = GROUP_WIDTH
    mix = _dot(oa_ref[...].astype(BF16), w_ref[0:gw, :])
    for n, r in enumerate((ob_ref, oc_ref, od_ref), start=1):
        mix = mix + _dot(r[...].astype(BF16), w_ref[n * gw:(n + 1) * gw, :])
    y_ref[...] = x_ref[...] + _rms(mix, g_ref[...])


def _out_proj(outs, w, g, x2, tm):
    t, d = x2.shape
    grp = pl.BlockSpec((tm, GROUP_WIDTH), lambda i: (i, 0))
    return pl.pallas_call(
        _out_proj_kernel,
        grid=(t // tm,),
        in_specs=[grp, grp, grp, grp,
                  pl.BlockSpec(w.shape, lambda i: (0, 0)),
                  pl.BlockSpec((1, d), lambda i: (0, 0)),
                  pl.BlockSpec((tm, d), lambda i: (i, 0))],
        out_specs=pl.BlockSpec((tm, d), lambda i: (i, 0)),
        out_shape=jax.ShapeDtypeStruct((t, d), F32),
        compiler_params=_cparams(("parallel",)),
        name="out_proj",
    )(*outs, w, g, x2)


def _gelu_tanh(x):
    return 0.5 * x * (1.0 + jnp.tanh(0.7978845608028654 * (x + 0.044715 * (x * x * x))))


def _ffn_kernel(x_ref, g1_ref, wg_ref, wu_ref, cw_ref, wd_ref, g2_ref, y_ref, prev_ref, *, fb):
    tm = x_ref.shape[1]
    f = wg_ref.shape[1]

    @pl.when(pl.program_id(1) == 0)
    def _():
        prev_ref[...] = jnp.zeros_like(prev_ref)

    x = x_ref[0]
    h = _rms(x, g1_ref[...]).astype(BF16)
    row = lax.broadcasted_iota(jnp.int32, (tm, fb), 0)
    acc = jnp.zeros((tm, x.shape[1]), F32)
    for n in range(f // fb):
        cs = slice(n * fb, (n + 1) * fb)
        gate = _dot(h, wg_ref[:, cs])
        up = _dot(h, wu_ref[:, cs])
        p1 = prev_ref[1:2, cs]
        p2 = prev_ref[0:1, cs]
        g1 = jnp.where(row == 0, p1, pltpu.roll(gate, 1, 0))
        g2 = jnp.where(row == 0, p2, jnp.where(row == 1, p1, pltpu.roll(gate, 2, 0)))
        prev_ref[0:2, cs] = gate[tm - 2:tm, :]
        conv = cw_ref[2:3, cs] * gate + cw_ref[1:2, cs] * g1 + cw_ref[0:1, cs] * g2
        acc = acc + _dot((_gelu_tanh(conv) * up).astype(BF16), wd_ref[cs, :])
    y_ref[0] = x + _rms(acc, g2_ref[...])


def _ffn(x, g1, wg, wu, cw, wd, g2, tm, fb):
    bsz, s, d = x.shape
    return pl.pallas_call(
        functools.partial(_ffn_kernel, fb=fb),
        grid=(bsz, s // tm),
        in_specs=[pl.BlockSpec((1, tm, d), lambda b, t: (b, t, 0)),
                  _row(g1), _row(wg), _row(wu), _row(cw), _row(wd), _row(g2)],
        out_specs=pl.BlockSpec((1, tm, d), lambda b, t: (b, t, 0)),
        out_shape=jax.ShapeDtypeStruct((bsz, s, d), F32),
        scratch_shapes=[pltpu.VMEM((SUBLANE, wg.shape[1]), F32)],
        compiler_params=_cparams(("parallel", "arbitrary")),
        name="ffn",
    )(x, g1, wg, wu, cw, wd, g2)


def _place(width, pieces):
    rows = pieces[0][1].shape[0]
    out = jnp.zeros((rows, width), pieces[0][1].dtype)
    for off, a in pieces:
        out = lax.dynamic_update_slice(out, a, (0, off))
    return out


def _rot_cols(w):
    half = w.shape[1] // 2
    return jnp.concatenate([-w[:, half:], w[:, :half]], axis=1)


def _in_proj_layout(w):
    gw, dkb = GROUP_WIDTH, N_HEADS * DK_B
    o_b = 4 * gw
    o_code = o_b + 2 * dkb + gw
    o_bg = o_code + GLA_GATE_RANK
    o_c = o_bg + gw
    o_kr = o_c + MLA_Q_RANK + MLA_KV_RANK
    o_d = o_kr + MLA_ROPE
    o_beta = o_d + 3 * gw
    o_a = o_beta + N_HEADS
    o_z = o_a + N_HEADS
    assert o_z + gw == w.shape[1]
    kr = w[:, o_kr:o_d]
    misc1 = _place(LANE, [(M_CODE, w[:, o_code:o_bg]), (M_BETA, w[:, o_beta:o_a]),
                          (M_A, w[:, o_a:o_z]), (M_KR, kr)])
    misc2 = _place(LANE, [(M_KR, _rot_cols(kr))])
    cols = [w[:, 0:o_b],
            w[:, o_b:o_code], w[:, o_bg:o_c],
            w[:, o_c:o_kr],
            w[:, o_d:o_beta], w[:, o_z:],
            misc1, misc2]
    widths = (4 * gw, 2 * dkb + 2 * gw, MLA_Q_RANK + MLA_KV_RANK, 4 * gw, LANE, LANE)
    return jnp.concatenate(cols, axis=1).astype(BF16), widths


def _mla_layout(w_uq, w_ukv):
    dq = MLA_NOPE + MLA_ROPE
    qa, qb, wk, wv = [], [], [], []
    for h in range(N_HEADS):
        nope = w_uq[:, h * dq:h * dq + MLA_NOPE]
        rope = w_uq[:, h * dq + MLA_NOPE:(h + 1) * dq]
        qa.append(_place(LANE, [(0, nope), (MLA_NOPE, rope)]))
        qb.append(_place(LANE, [(MLA_NOPE, _rot_cols(rope))]))
        kv = w_ukv[:, h * 2 * HEAD_DIM:(h + 1) * 2 * HEAD_DIM]
        wk.append(_place(LANE, [(0, kv[:, :MLA_NOPE])]))
        wv.append(kv[:, MLA_NOPE:])
    cat = lambda xs: jnp.concatenate(xs, axis=1).astype(BF16)
    return cat(qa), cat(qb), cat(wk), cat(wv)


def _rope_tables(s):
    inv = ROPE_THETA ** (-jnp.arange(0, MLA_ROPE, 2, dtype=F32) / MLA_ROPE)
    ang = jnp.arange(s, dtype=F32)[:, None] * inv[None, :]
    cos2 = jnp.concatenate([jnp.cos(ang), jnp.cos(ang)], axis=1)
    sin2 = jnp.concatenate([jnp.sin(ang), jnp.sin(ang)], axis=1)
    ones = jnp.ones((s, MLA_NOPE), F32)
    cosq = _place(LANE, [(0, ones), (MLA_NOPE, cos2)])
    sinq = _place(LANE, [(MLA_NOPE, sin2)])
    cosk = _place(LANE, [(MLA_NOPE, cos2)])
    return cosq, sinq, cosk


def _tile_heads(g):
    return jnp.tile(g, N_HEADS)[None, :]


def _pick(n, prefs):
    for p in prefs:
        if n % p == 0:
            return p
    return n


def kernel(x, w_in, w_out, pre_mix_g, post_mix_g, pre_ffn_g, post_ffn_g, hgrn_lb_logits, hgrn_norm_g,
           gla_w_gk2, gla_b_gk, gla_norm_g, mla_q_norm_g, mla_w_uq, mla_kv_norm_g, mla_w_ukv,
           gdn_conv_w, gdn_a_log, gdn_dt_bias, gdn_norm_g, ffn_w_gate, ffn_w_up, ffn_conv_w, ffn_w_down):
    bsz, s, d = x.shape
    depth = w_in.shape[0]
    t = bsz * s
    tm = _pick(t, (256,))
    tt = _pick(s, (256,))
    tq = _pick(s, (512, 256, 128))
    fb = 256

    cosq, sinq, cosk = _rope_tables(s)
    p = jax.nn.softmax(hgrn_lb_logits.astype(F32), axis=0)
    lower_bounds = jnp.cumsum(p, axis=0) - p[0]

    lanes = jnp.arange(LANE)
    head_of = jnp.arange(GROUP_WIDTH) // HEAD_DIM
    ebeta = (lanes[:, None] == M_BETA + head_of[None, :]).astype(F32)
    ea = (lanes[:, None] == M_A + head_of[None, :]).astype(F32)

    for l in range(depth):
        w_in_l, widths = _in_proj_layout(w_in[l])
        za, zb, zc, zd, zm, zm2 = _in_proj(x.reshape(t, d), pre_mix_g[l][None, :], w_in_l, widths, tm)
        r3 = lambda a: a.reshape(bsz, s, a.shape[-1])
        za, zb, zc, zd, zm, zm2 = map(r3, (za, zb, zc, zd, zm, zm2))

        o_a = _hgrn(za, lower_bounds[l][None, :], hgrn_norm_g[l][None, :], tt)

        wgk = _place(N_HEADS * DK_B, [(0, gla_w_gk2[l])])
        wgk = jnp.concatenate([wgk, jnp.zeros((LANE - GLA_GATE_RANK, N_HEADS * DK_B), F32)], axis=0).astype(BF16)
        o_b = _gla(zb, zm, wgk, gla_b_gk[l][None, :], gla_norm_g[l][None, :], tt)

        wqa, wqb, wk, wv = _mla_layout(mla_w_uq[l], mla_w_ukv[l])
        qh, kh, vh = _mla_pre(zc, zm, zm2, mla_q_norm_g[l][None, :], mla_kv_norm_g[l][None, :],
                              wqa, wqb, wk, wv, cosq, sinq, cosk, tt)
        o_c = _mla_attn(qh, kh, vh, tq)

        apad = _place(LANE, [(M_A, jnp.exp(gdn_a_log[l].astype(F32))[None, :])])
        dtpad = _place(LANE, [(M_A, gdn_dt_bias[l][None, :])])
        o_d = _gdn(zd, zm, gdn_conv_w[l], apad, dtpad, ebeta, ea, gdn_norm_g[l][None, :], tt)

        outs = [o.reshape(t, GROUP_WIDTH) for o in (o_a, o_b, o_c, o_d)]
        x = _out_proj(outs, w_out[l].astype(BF16), post_mix_g[l][None, :], x.reshape(t, d), tm).reshape(bsz, s, d)

        x = _ffn(x, pre_ffn_g[l][None, :], ffn_w_gate[l].astype(BF16), ffn_w_up[l].astype(BF16),
                 ffn_conv_w[l], ffn_w_down[l].astype(BF16), post_ffn_g[l][None, :], tm=_pick(s, (256,)), fb=fb)
    return x
```

```python
import functools

import jax
import jax.numpy as jnp
from jax import lax
from jax.experimental import pallas as pl
from jax.experimental.pallas import tpu as pltpu

F32 = jnp.float32
BF16 = jnp.bfloat16
HIGHEST = lax.Precision.HIGHEST

N_GROUPS = 4
HEAD_DIM = 64
N_HEADS = 4
GROUP_WIDTH = N_HEADS * HEAD_DIM
EPS = 1e-6
MASK_VALUE = -1e30
MIN_POS = 1e-30
LOG2E = 1.4426950408889634

DK_B = HEAD_DIM // 2
GLA_GATE_RANK = 16
GLA_GATE_NORMALIZER = 16.0
MLA_Q_RANK = 256
MLA_KV_RANK = 128
MLA_NOPE = 64
MLA_ROPE = 32
ROPE_THETA = 10000.0
GDN_CONV = 4
FFN_CONV = 3
LANE = 128
SUBLANE = 8

M_CODE = 0
M_BETA = 16
M_A = 20
M_KR = 64

BAND = 16
GDN_CHUNK = 64
VMEM_LIMIT = 56 * 1024 * 1024


def _cparams(sem):
    return pltpu.CompilerParams(dimension_semantics=sem, vmem_limit_bytes=VMEM_LIMIT)


def _rms(x, g):
    return x * lax.rsqrt(jnp.mean(x * x, axis=-1, keepdims=True) + EPS) * g


def _sigmoid(x):
    return jax.nn.sigmoid(x)


def _softplus(x):
    return jnp.maximum(x, 0.0) + jnp.log1p(jnp.exp(-jnp.abs(x)))


def _block_cumsum(x, rowmod, block):
    s = 1
    while s < block:
        x = x + jnp.where(rowmod >= s, pltpu.roll(x, s, 0), 0.0)
        s *= 2
    return x


def _dot(a, b, dims=(((1,), (0,)), ((), ())), precision=None):
    return lax.dot_general(a, b, dims, precision=precision, preferred_element_type=F32)


NT = (((1,), (1,)), ((), ()))
TN = (((0,), (0,)), ((), ()))


def _in_proj_kernel(x_ref, g_ref, w_ref, *out_refs):
    h = _rms(x_ref[...], g_ref[...]).astype(BF16)
    off = 0
    for o_ref in out_refs:
        n = o_ref.shape[-1]
        o_ref[...] = _dot(h, w_ref[:, off:off + n])
        off += n


def _in_proj(x2, g, w, widths, tm):
    t, d = x2.shape
    n = w.shape[1]
    return pl.pallas_call(
        _in_proj_kernel,
        grid=(t // tm,),
        in_specs=[pl.BlockSpec((tm, d), lambda i: (i, 0)),
                  pl.BlockSpec((1, d), lambda i: (0, 0)),
                  pl.BlockSpec((d, n), lambda i: (0, 0))],
        out_specs=[pl.BlockSpec((tm, wd), lambda i: (i, 0)) for wd in widths],
        out_shape=[jax.ShapeDtypeStruct((t, wd), F32) for wd in widths],
        compiler_params=_cparams(("parallel",)),
        name="in_proj",
    )(x2, g, w)


def _gla_core(q, k, v, lg, s_ref, oacc_ref, kv_ref, sc_ref, dkh):
    tt, dk = q.shape
    dv = v.shape[1]
    rowmod_k = lax.broadcasted_iota(jnp.int32, (tt, dk), 0) & (BAND - 1)
    b = _block_cumsum(lg * LOG2E, rowmod_k, BAND)

    seg = (lax.broadcasted_iota(jnp.int32, (dk, dv), 0) // dkh
           == lax.broadcasted_iota(jnp.int32, (dk, dv), 1) // HEAD_DIM).astype(BF16)
    o = _dot((q * k).astype(BF16), seg) * v
    for d in range(1, BAND):
        kd = pltpu.roll(k, d, 0)
        bd = pltpu.roll(b, d, 0)
        vd = pltpu.roll(v, d, 0)
        p = q * kd * jnp.exp2(jnp.where(rowmod_k >= d, b - bd, MASK_VALUE))
        o = o + _dot(p.astype(BF16), seg) * vd

    nblk = tt // BAND
    qe = (q * jnp.exp2(b)).astype(BF16)
    for n in range(nblk):
        r0 = n * BAND
        bn = b[r0:r0 + BAND]
        kt = (k[r0:r0 + BAND] * jnp.exp2(bn[BAND - 1:BAND] - bn)).astype(BF16)
        kv_ref[n] = _dot(v[r0:r0 + BAND].astype(BF16), kt, TN)
    for h in range(N_HEADS):
        hs = slice(h * HEAD_DIM, (h + 1) * HEAD_DIM)
        l0 = (h * dkh // LANE) * LANE
        lt = slice(l0, l0 + LANE)
        own = (lax.broadcasted_iota(jnp.int32, (HEAD_DIM, LANE), 1) + l0) // dkh == h
        st = s_ref[hs, lt]
        for n in range(nblk):
            sc_ref[n, hs, lt] = jnp.where(own, st, 0.0).astype(BF16)
            st = st * jnp.exp2(b[n * BAND + BAND - 1:(n + 1) * BAND, lt]) + kv_ref[n, hs, lt]
        s_ref[hs, lt] = st
    for n in range(nblk):
        r0 = n * BAND
        oacc_ref[r0:r0 + BAND, :] = _dot(qe[r0:r0 + BAND], sc_ref[n], NT)
    return o + oacc_ref[...]


def _head_norm_gate(o, ng, gate_act, o_ref):
    for h in range(N_HEADS):
        sl = slice(h * HEAD_DIM, (h + 1) * HEAD_DIM)
        o_ref[0, :, sl] = _rms(o[:, sl], ng) * gate_act[:, sl]


def _hgrn_kernel(z_ref, lb_ref, ng_ref, o_ref, s_ref, oacc_ref, kv_ref, sc_ref):
    @pl.when(pl.program_id(1) == 0)
    def _():
        s_ref[...] = jnp.zeros_like(s_ref)
        sc_ref[...] = jnp.zeros_like(sc_ref)

    gw = GROUP_WIDTH
    q = z_ref[0, :, 0:gw]
    zf = z_ref[0, :, gw:2 * gw]
    v = z_ref[0, :, 2 * gw:3 * gw]
    gate = z_ref[0, :, 3 * gw:4 * gw]
    lb = lb_ref[...]
    f = lb + (1.0 - lb) * _sigmoid(zf)
    lg = jnp.log(jnp.maximum(f, MIN_POS))
    k = (1.0 - lb) * _sigmoid(-zf)
    qh = q * _sigmoid(q) * HEAD_DIM ** -0.5
    o = _gla_core(qh, k, v, lg, s_ref, oacc_ref, kv_ref, sc_ref, HEAD_DIM)
    _head_norm_gate(o, ng_ref[...], _sigmoid(gate), o_ref)


def _gla_kernel(z_ref, m_ref, wgk_ref, bgk_ref, ng_ref, o_ref, s_ref, oacc_ref, kv_ref, sc_ref):
    @pl.when(pl.program_id(1) == 0)
    def _():
        s_ref[...] = jnp.zeros_like(s_ref)
        sc_ref[...] = jnp.zeros_like(sc_ref)

    dk = N_HEADS * DK_B
    q = z_ref[0, :, 0:dk]
    k = z_ref[0, :, dk:2 * dk]
    v = z_ref[0, :, 2 * dk:2 * dk + GROUP_WIDTH]
    gate = z_ref[0, :, 2 * dk + GROUP_WIDTH:2 * dk + 2 * GROUP_WIDTH]
    x = _dot(m_ref[0].astype(BF16), wgk_ref[...]) + bgk_ref[...]
    lg = -_softplus(-x) / GLA_GATE_NORMALIZER
    o = _gla_core(q * DK_B ** -0.5, k, v, lg, s_ref, oacc_ref, kv_ref, sc_ref, DK_B)
    _head_norm_gate(o, ng_ref[...], gate * _sigmoid(gate), o_ref)


def _row(a):
    return pl.BlockSpec(a.shape, lambda b, t: (0,) * a.ndim)


def _gla_scratch(tt, dk):
    nblk = tt // BAND
    return [pltpu.VMEM((GROUP_WIDTH, dk), F32),
            pltpu.VMEM((tt, GROUP_WIDTH), F32),
            pltpu.VMEM((nblk, GROUP_WIDTH, dk), F32),
            pltpu.VMEM((nblk, GROUP_WIDTH, dk), BF16)]


def _hgrn(za, lb, ng, tt):
    bsz, s, _ = za.shape
    return pl.pallas_call(
        _hgrn_kernel,
        grid=(bsz, s // tt),
        in_specs=[pl.BlockSpec((1, tt, za.shape[2]), lambda b, t: (b, t, 0)), _row(lb), _row(ng)],
        out_specs=pl.BlockSpec((1, tt, GROUP_WIDTH), lambda b, t: (b, t, 0)),
        out_shape=jax.ShapeDtypeStruct((bsz, s, GROUP_WIDTH), F32),
        scratch_shapes=_gla_scratch(tt, GROUP_WIDTH),
        compiler_params=_cparams(("parallel", "arbitrary")),
        name="hgrn",
    )(za, lb, ng)


def _gla(zb, zm, wgk, bgk, ng, tt):
    bsz, s, _ = zb.shape
    return pl.pallas_call(
        _gla_kernel,
        grid=(bsz, s // tt),
        in_specs=[pl.BlockSpec((1, tt, zb.shape[2]), lambda b, t: (b, t, 0)),
                  pl.BlockSpec((1, tt, LANE), lambda b, t: (b, t, 0)),
                  _row(wgk), _row(bgk), _row(ng)],
        out_specs=pl.BlockSpec((1, tt, GROUP_WIDTH), lambda b, t: (b, t, 0)),
        out_shape=jax.ShapeDtypeStruct((bsz, s, GROUP_WIDTH), F32),
        scratch_shapes=_gla_scratch(tt, N_HEADS * DK_B),
        compiler_params=_cparams(("parallel", "arbitrary")),
        name="gla",
    )(zb, zm, wgk, bgk, ng)


def _gdn_kernel(z_ref, m_ref, cw_ref, apad_ref, dtpad_ref, ebeta_ref, ea_ref, ng_ref,
                o_ref, xs_ref, s_ref, oacc_ref, u_ref, w_ref, aqk_ref, sc_ref):
    tt = z_ref.shape[1]
    gw = GROUP_WIDTH
    c = GDN_CHUNK
    halo = SUBLANE

    @pl.when(pl.program_id(1) == 0)
    def _():
        s_ref[...] = jnp.zeros_like(s_ref)
        xs_ref[0:halo, :] = jnp.zeros((halo, 3 * gw), F32)

    xs_ref[halo:halo + tt, :] = z_ref[0, :, 0:3 * gw]
    acc = cw_ref[GDN_CONV - 1:GDN_CONV, :] * xs_ref[halo:halo + tt, :]
    for j in range(1, GDN_CONV):
        acc = acc + cw_ref[GDN_CONV - 1 - j:GDN_CONV - j, :] * xs_ref[halo - j:halo - j + tt, :]
    xs_ref[0:halo, :] = xs_ref[tt:tt + halo, :]
    qkv = acc * _sigmoid(acc)

    q_all, k_all, v_all = qkv[:, 0:gw], qkv[:, gw:2 * gw], qkv[:, 2 * gw:3 * gw]

    head_blk = (lax.broadcasted_iota(jnp.int32, (gw, gw), 0) // HEAD_DIM
                == lax.broadcasted_iota(jnp.int32, (gw, gw), 1) // HEAD_DIM)
    seg = head_blk.astype(BF16)

    def head_sum(x):
        hi = x.astype(BF16)
        lo = (x - hi.astype(F32)).astype(BF16)
        return _dot(hi, seg) + _dot(lo, seg)

    def expand(x, e_ref, terms):
        acc, r = None, x
        for _ in range(terms):
            hi = r.astype(BF16)
            r = r - hi.astype(F32)
            part = _dot(hi, e_ref[...])
            acc = part if acc is None else acc + part
        return acc

    qn = q_all * lax.rsqrt(head_sum(q_all * q_all) + EPS) * HEAD_DIM ** -0.5
    kn = k_all * lax.rsqrt(head_sum(k_all * k_all) + EPS)

    m = m_ref[0]
    beta_x = expand(_sigmoid(m), ebeta_ref, 2)
    lg_m = -apad_ref[...] * _softplus(m + dtpad_ref[...])
    rowmod_m = lax.broadcasted_iota(jnp.int32, (tt, LANE), 0) & (c - 1)
    b_m = _block_cumsum(lg_m, rowmod_m, c)
    b_x = expand(b_m, ea_ref, 3)
    eb = jnp.exp(b_x)
    kb = kn * beta_x
    vb = v_all * beta_x
    kbe = kb * eb
    qe = qn * eb

    ii = lax.broadcasted_iota(jnp.int32, (tt, tt), 0)
    jj = lax.broadcasted_iota(jnp.int32, (tt, tt), 1)
    same = (ii // c) == (jj // c)
    incl = same & (ii >= jj)
    strict = same & (ii > jj)

    for h in range(N_HEADS):
        sl = slice(h * HEAD_DIM, (h + 1) * HEAD_DIM)
        bmat = jnp.broadcast_to(b_m[:, M_A + h:M_A + h + 1], (tt, tt))
        lmask = jnp.where(incl, jnp.exp(jnp.minimum(bmat - bmat.T, 0.0)), 0.0)
        kh16 = kn[:, sl].astype(BF16)
        nm = jnp.where(strict, -_dot(kb[:, sl].astype(BF16), kh16, NT) * lmask, 0.0)
        tn = nm
        pw = nm
        for _ in range(5):
            pw16 = pw.astype(BF16)
            pw = _dot(pw16, pw16)
            tn = tn + pw + _dot(tn.astype(BF16), pw.astype(BF16))
        tn16 = tn.astype(BF16)
        u_ref[:, sl] = vb[:, sl] + _dot(tn16, vb[:, sl].astype(BF16))
        w_ref[:, sl] = kbe[:, sl] + _dot(tn16, kbe[:, sl].astype(BF16))
        aqk_ref[h] = (_dot(qn[:, sl].astype(BF16), kh16, NT) * lmask).astype(BF16)

    st = s_ref[...]
    for n in range(tt // c):
        rs = slice(n * c, (n + 1) * c)
        b_last = b_x[n * c + c - 1:(n + 1) * c, :]
        kt_t = (kn[rs] * jnp.exp(b_last - b_x[rs])).T.astype(BF16)
        pm = jnp.where(head_blk, _dot(kt_t, w_ref[rs, :].astype(BF16)), 0.0)
        rm = jnp.where(head_blk, _dot(kt_t, u_ref[rs, :].astype(BF16)), 0.0)
        st16 = st.astype(BF16)
        sc_ref[n] = st16
        st = st * jnp.exp(b_last) - _dot(pm.astype(BF16), st16) + rm
    s_ref[...] = st

    for n in range(tt // c):
        rs = slice(n * c, (n + 1) * c)
        lhs = jnp.concatenate([w_ref[rs, :], qe[rs]], axis=0).astype(BF16)
        res = _dot(lhs, sc_ref[n])
        u_ref[rs, :] = u_ref[rs, :] - res[0:c]
        oacc_ref[rs, :] = res[c:2 * c]
    for h in range(N_HEADS):
        sl = slice(h * HEAD_DIM, (h + 1) * HEAD_DIM)
        oacc_ref[:, sl] = oacc_ref[:, sl] + _dot(aqk_ref[h], u_ref[:, sl].astype(BF16))

    zg = z_ref[0, :, 3 * gw:4 * gw]
    _head_norm_gate(oacc_ref[...], ng_ref[...], zg * _sigmoid(zg), o_ref)


def _gdn(zd, zm, cw, apad, dtpad, ebeta, ea, ng, tt):
    bsz, s, _ = zd.shape
    gw = GROUP_WIDTH
    return pl.pallas_call(
        _gdn_kernel,
        grid=(bsz, s // tt),
        in_specs=[pl.BlockSpec((1, tt, zd.shape[2]), lambda b, t: (b, t, 0)),
                  pl.BlockSpec((1, tt, LANE), lambda b, t: (b, t, 0)),
                  _row(cw), _row(apad), _row(dtpad), _row(ebeta), _row(ea), _row(ng)],
        out_specs=pl.BlockSpec((1, tt, gw), lambda b, t: (b, t, 0)),
        out_shape=jax.ShapeDtypeStruct((bsz, s, gw), F32),
        scratch_shapes=[pltpu.VMEM((tt + 2 * SUBLANE, 3 * gw), F32),
                        pltpu.VMEM((gw, gw), F32),
                        pltpu.VMEM((tt, gw), F32),
                        pltpu.VMEM((tt, gw), F32),
                        pltpu.VMEM((tt, gw), F32),
                        pltpu.VMEM((N_HEADS, tt, tt), BF16),
                        pltpu.VMEM((tt // GDN_CHUNK, gw, gw), BF16)],
        compiler_params=_cparams(("parallel", "arbitrary")),
        name="gdn",
    )(zd, zm, cw, apad, dtpad, ebeta, ea, ng)


def _mla_pre_kernel(zc_ref, m1_ref, m2_ref, gq_ref, gkv_ref, wqa_ref, wqb_ref, wk_ref, wv_ref,
                    cosq_ref, sinq_ref, cosk_ref, qt_ref, k_ref, vt_ref):
    scale = (MLA_NOPE + MLA_ROPE) ** -0.5
    cq = _rms(zc_ref[0, :, 0:MLA_Q_RANK], gq_ref[...]).astype(BF16)
    ckv = _rms(zc_ref[0, :, MLA_Q_RANK:MLA_Q_RANK + MLA_KV_RANK], gkv_ref[...]).astype(BF16)
    qa = _dot(cq, wqa_ref[...])
    qb = _dot(cq, wqb_ref[...])
    kn = _dot(ckv, wk_ref[...])
    vv = _dot(ckv, wv_ref[...])
    cosq, sinq = cosq_ref[...], sinq_ref[...]
    kr = m1_ref[0] * cosk_ref[...] + m2_ref[0] * sinq
    for h in range(N_HEADS):
        sl = slice(h * LANE, (h + 1) * LANE)
        q = (qa[:, sl] * cosq + qb[:, sl] * sinq) * scale
        qt_ref[0, sl, :] = q.T.astype(BF16)
        k_ref[0, h] = (kn[:, sl] + kr).astype(BF16)
    vt_ref[0] = vv.T.astype(BF16)


def _mla_pre(zc, zm, zm2, gq, gkv, wqa, wqb, wk, wv, cosq, sinq, cosk, tm):
    bsz, s, _ = zc.shape
    tok = lambda wd: pl.BlockSpec((1, tm, wd), lambda b, t: (b, t, 0))
    tab = pl.BlockSpec((tm, LANE), lambda b, t: (t, 0))
    tr = lambda rows: pl.BlockSpec((1, rows, tm), lambda b, t: (b, 0, t))
    return pl.pallas_call(
        _mla_pre_kernel,
        grid=(bsz, s // tm),
        in_specs=[tok(zc.shape[2]), tok(LANE), tok(LANE), _row(gq), _row(gkv),
                  _row(wqa), _row(wqb), _row(wk), _row(wv), tab, tab, tab],
        out_specs=[tr(N_HEADS * LANE),
                   pl.BlockSpec((1, N_HEADS, tm, LANE), lambda b, t: (b, 0, t, 0)),
                   tr(GROUP_WIDTH)],
        out_shape=[jax.ShapeDtypeStruct((bsz, N_HEADS * LANE, s), BF16),
                   jax.ShapeDtypeStruct((bsz, N_HEADS, s, LANE), BF16),
                   jax.ShapeDtypeStruct((bsz, GROUP_WIDTH, s), BF16)],
        compiler_params=_cparams(("parallel", "parallel")),
        name="mla_pre",
    )(zc, zm, zm2, gq, gkv, wqa, wqb, wk, wv, cosq, sinq, cosk)


def _mla_attn_kernel(qi_ref, kj_ref, qt_ref, k_ref, vt_ref, o_ref, m_ref, l_ref, acc_ref):
    pair = pl.program_id(1)
    i = qi_ref[pair]
    j = kj_ref[pair]
    tq = qt_ref.shape[2]
    tk = k_ref.shape[2]

    @pl.when(j == 0)
    def _():
        m_ref[...] = jnp.full_like(m_ref, MASK_VALUE)
        l_ref[...] = jnp.zeros_like(l_ref)
        acc_ref[...] = jnp.zeros_like(acc_ref)

    def step(masked):
        for h in range(N_HEADS):
            s = _dot(k_ref[0, h], qt_ref[0, h * LANE:(h + 1) * LANE, :])
            if masked:
                keep = (lax.broadcasted_iota(jnp.int32, (tk, tq), 0)
                        <= lax.broadcasted_iota(jnp.int32, (tk, tq), 1))
                s = jnp.where(keep, s, MASK_VALUE)
            m_prev = m_ref[h:h + 1, :]
            m_new = jnp.maximum(m_prev, jnp.max(s, axis=0, keepdims=True))
            alpha = jnp.exp(m_prev - m_new)
            p = jnp.exp(s - m_new)
            l_ref[h:h + 1, :] = alpha * l_ref[h:h + 1, :] + jnp.sum(p, axis=0, keepdims=True)
            hs = slice(h * HEAD_DIM, (h + 1) * HEAD_DIM)
            acc_ref[hs, :] = alpha * acc_ref[hs, :] + _dot(vt_ref[0, hs, :], p.astype(BF16))
            m_ref[h:h + 1, :] = m_new

    @pl.when(j < i)
    def _():
        step(False)

    @pl.when(j == i)
    def _():
        step(True)
        for h in range(N_HEADS):
            hs = slice(h * HEAD_DIM, (h + 1) * HEAD_DIM)
            acc_ref[hs, :] = acc_ref[hs, :] / l_ref[h:h + 1, :]
        o_ref[0] = acc_ref[...].T


def _mla_attn(qt, k, vt, tq):
    bsz, _, s = qt.shape
    nq = s // tq
    pairs = [(i, j) for i in range(nq) for j in range(i + 1)]
    qi = jnp.array([p[0] for p in pairs], jnp.int32)
    kj = jnp.array([p[1] for p in pairs], jnp.int32)
    grid_spec = pltpu.PrefetchScalarGridSpec(
        num_scalar_prefetch=2,
        grid=(bsz, len(pairs)),
        in_specs=[pl.BlockSpec((1, N_HEADS * LANE, tq), lambda b, p, qi, kj: (b, 0, qi[p])),
                  pl.BlockSpec((1, N_HEADS, tq, LANE), lambda b, p, qi, kj: (b, 0, kj[p], 0)),
                  pl.BlockSpec((1, GROUP_WIDTH, tq), lambda b, p, qi, kj: (b, 0, kj[p]))],
        out_specs=pl.BlockSpec((1, tq, GROUP_WIDTH), lambda b, p, qi, kj: (b, qi[p], 0)),
        scratch_shapes=[pltpu.VMEM((SUBLANE, tq), F32), pltpu.VMEM((SUBLANE, tq), F32),
                        pltpu.VMEM((GROUP_WIDTH, tq), F32)],
    )
    return pl.pallas_call(
        _mla_attn_kernel,
        grid_spec=grid_spec,
        out_shape=jax.ShapeDtypeStruct((bsz, s, GROUP_WIDTH), F32),
        compiler_params=_cparams(("parallel", "arbitrary")),
        name="mla_attn",
    )(qi, kj, qt, k, vt)


def _out_proj_kernel(oa_ref, ob_ref, oc_ref, od_ref, w_ref, g_ref, x_ref, y_ref):
    gw = GROUP_WIDTH
    mix = _dot(oa_ref[...].astype(BF16), w_ref[0:gw, :])
    for n, r in enumerate((ob_ref, oc_ref, od_ref), start=1):
        mix = mix + _dot(r[...].astype(BF16), w_ref[n * gw:(n + 1) * gw, :])
    y_ref[...] = x_ref[...] + _rms(mix, g_ref[...])


def _out_proj(outs, w, g, x2, tm):
    t, d = x2.shape
    grp = pl.BlockSpec((tm, GROUP_WIDTH), lambda i: (i, 0))
    return pl.pallas_call(
        _out_proj_kernel,
        grid=(t // tm,),
        in_specs=[grp, grp, grp, grp,
                  pl.BlockSpec(w.shape, lambda i: (0, 0)),
                  pl.BlockSpec((1, d), lambda i: (0, 0)),
                  pl.BlockSpec((tm, d), lambda i: (i, 0))],
        out_specs=pl.BlockSpec((tm, d), lambda i: (i, 0)),
        out_shape=jax.ShapeDtypeStruct((t, d), F32),
        compiler_params=_cparams(("parallel",)),
        name="out_proj",
    )(*outs, w, g, x2)


def _gelu_tanh(x):
    return 0.5 * x * (1.0 + jnp.tanh(0.7978845608028654 * (x + 0.044715 * (x * x * x))))


def _ffn_kernel(x_ref, g1_ref, wg_ref, wu_ref, cw_ref, wd_ref, g2_ref, y_ref, prev_ref, act_ref, *, fb):
    tm = x_ref.shape[1]
    f = wg_ref.shape[1]

    @pl.when(pl.program_id(1) == 0)
    def _():
        prev_ref[...] = jnp.zeros_like(prev_ref)

    x = x_ref[0]
    h = _rms(x, g1_ref[...]).astype(BF16)
    row = lax.broadcasted_iota(jnp.int32, (tm, fb), 0)
    for n in range(f // fb):
        cs = slice(n * fb, (n + 1) * fb)
        gate = _dot(h, wg_ref[:, cs])
        up = _dot(h, wu_ref[:, cs])
        p1 = prev_ref[1:2, cs]
        p2 = prev_ref[0:1, cs]
        g1 = jnp.where(row == 0, p1, pltpu.roll(gate, 1, 0))
        g2 = jnp.where(row == 0, p2, jnp.where(row == 1, p1, pltpu.roll(gate, 2, 0)))
        prev_ref[0:2, cs] = gate[tm - 2:tm, :]
        conv = cw_ref[2:3, cs] * gate + cw_ref[1:2, cs] * g1 + cw_ref[0:1, cs] * g2
        act_ref[:, cs] = (_gelu_tanh(conv) * up).astype(BF16)
    y_ref[0] = x + _rms(_dot(act_ref[...], wd_ref[...]), g2_ref[...])


def _ffn(x, g1, wg, wu, cw, wd, g2, tm, fb):
    bsz, s, d = x.shape
    once = lambda a: pl.BlockSpec(a.shape, lambda b, t: (0,) * a.ndim, pipeline_mode=pl.Buffered(1))
    return pl.pallas_call(
        functools.partial(_ffn_kernel, fb=fb),
        grid=(bsz, s // tm),
        in_specs=[pl.BlockSpec((1, tm, d), lambda b, t: (b, t, 0)),
                  _row(g1), once(wg), once(wu), _row(cw), once(wd), _row(g2)],
        out_specs=pl.BlockSpec((1, tm, d), lambda b, t: (b, t, 0)),
        out_shape=jax.ShapeDtypeStruct((bsz, s, d), F32),
        scratch_shapes=[pltpu.VMEM((SUBLANE, wg.shape[1]), F32),
                        pltpu.VMEM((tm, wg.shape[1]), BF16)],
        compiler_params=_cparams(("parallel", "arbitrary")),
        name="ffn",
    )(x, g1, wg, wu, cw, wd, g2)


def _place(width, pieces):
    rows = pieces[0][1].shape[0]
    out = jnp.zeros((rows, width), pieces[0][1].dtype)
    for off, a in pieces:
        out = lax.dynamic_update_slice(out, a, (0, off))
    return out


def _rot_cols(w):
    half = w.shape[1] // 2
    return jnp.concatenate([-w[:, half:], w[:, :half]], axis=1)


def _in_proj_layout(w):
    gw, dkb = GROUP_WIDTH, N_HEADS * DK_B
    o_b = 4 * gw
    o_code = o_b + 2 * dkb + gw
    o_bg = o_code + GLA_GATE_RANK
    o_c = o_bg + gw
    o_kr = o_c + MLA_Q_RANK + MLA_KV_RANK
    o_d = o_kr + MLA_ROPE
    o_beta = o_d + 3 * gw
    o_a = o_beta + N_HEADS
    o_z = o_a + N_HEADS
    assert o_z + gw == w.shape[1]
    kr = w[:, o_kr:o_d]
    misc1 = _place(LANE, [(M_CODE, w[:, o_code:o_bg]), (M_BETA, w[:, o_beta:o_a]),
                          (M_A, w[:, o_a:o_z]), (M_KR, kr)])
    misc2 = _place(LANE, [(M_KR, _rot_cols(kr))])
    cols = [w[:, 0:o_b],
            w[:, o_b:o_code], w[:, o_bg:o_c],
            w[:, o_c:o_kr],
            w[:, o_d:o_beta], w[:, o_z:],
            misc1, misc2]
    widths = (4 * gw, 2 * dkb + 2 * gw, MLA_Q_RANK + MLA_KV_RANK, 4 * gw, LANE, LANE)
    return jnp.concatenate(cols, axis=1).astype(BF16), widths


def _mla_layout(w_uq, w_ukv):
    dq = MLA_NOPE + MLA_ROPE
    qa, qb, wk, wv = [], [], [], []
    for h in range(N_HEADS):
        nope = w_uq[:, h * dq:h * dq + MLA_NOPE]
        rope = w_uq[:, h * dq + MLA_NOPE:(h + 1) * dq]
        qa.append(_place(LANE, [(0, nope), (MLA_NOPE, rope)]))
        qb.append(_place(LANE, [(MLA_NOPE, _rot_cols(rope))]))
        kv = w_ukv[:, h * 2 * HEAD_DIM:(h + 1) * 2 * HEAD_DIM]
        wk.append(_place(LANE, [(0, kv[:, :MLA_NOPE])]))
        wv.append(kv[:, MLA_NOPE:])
    cat = lambda xs: jnp.concatenate(xs, axis=1).astype(BF16)
    return cat(qa), cat(qb), cat(wk), cat(wv)


def _rope_tables(s):
    inv = ROPE_THETA ** (-jnp.arange(0, MLA_ROPE, 2, dtype=F32) / MLA_ROPE)
    ang = jnp.arange(s, dtype=F32)[:, None] * inv[None, :]
    cos2 = jnp.concatenate([jnp.cos(ang), jnp.cos(ang)], axis=1)
    sin2 = jnp.concatenate([jnp.sin(ang), jnp.sin(ang)], axis=1)
    ones = jnp.ones((s, MLA_NOPE), F32)
    cosq = _place(LANE, [(0, ones), (MLA_NOPE, cos2)])
    sinq = _place(LANE, [(MLA_NOPE, sin2)])
    cosk = _place(LANE, [(MLA_NOPE, cos2)])
    return cosq, sinq, cosk


def _tile_heads(g):
    return jnp.tile(g, N_HEADS)[None, :]


def _pick(n, prefs):
    for p in prefs:
        if n % p == 0:
            return p
    return n


def kernel(x, w_in, w_out, pre_mix_g, post_mix_g, pre_ffn_g, post_ffn_g, hgrn_lb_logits, hgrn_norm_g,
           gla_w_gk2, gla_b_gk, gla_norm_g, mla_q_norm_g, mla_w_uq, mla_kv_norm_g, mla_w_ukv,
           gdn_conv_w, gdn_a_log, gdn_dt_bias, gdn_norm_g, ffn_w_gate, ffn_w_up, ffn_conv_w, ffn_w_down):
    bsz, s, d = x.shape
    depth = w_in.shape[0]
    t = bsz * s
    tm = _pick(t, (512, 256))
    tt = _pick(s, (256,))
    tq = _pick(s, (512, 256, 128))
    fb = 256

    cosq, sinq, cosk = _rope_tables(s)
    p = jax.nn.softmax(hgrn_lb_logits.astype(F32), axis=0)
    lower_bounds = jnp.cumsum(p, axis=0) - p[0]

    lanes = jnp.arange(LANE)
    head_of = jnp.arange(GROUP_WIDTH) // HEAD_DIM
    ebeta = (lanes[:, None] == M_BETA + head_of[None, :]).astype(BF16)
    ea = (lanes[:, None] == M_A + head_of[None, :]).astype(BF16)

    for l in range(depth):
        w_in_l, widths = _in_proj_layout(w_in[l])
        za, zb, zc, zd, zm, zm2 = _in_proj(x.reshape(t, d), pre_mix_g[l][None, :], w_in_l, widths, tm)
        r3 = lambda a: a.reshape(bsz, s, a.shape[-1])
        za, zb, zc, zd, zm, zm2 = map(r3, (za, zb, zc, zd, zm, zm2))

        o_a = _hgrn(za, lower_bounds[l][None, :], hgrn_norm_g[l][None, :], tt)

        wgk = _place(N_HEADS * DK_B, [(0, gla_w_gk2[l])])
        wgk = jnp.concatenate([wgk, jnp.zeros((LANE - GLA_GATE_RANK, N_HEADS * DK_B), F32)], axis=0).astype(BF16)
        o_b = _gla(zb, zm, wgk, gla_b_gk[l][None, :], gla_norm_g[l][None, :], tt)

        wqa, wqb, wk, wv = _mla_layout(mla_w_uq[l], mla_w_ukv[l])
        qh, kh, vh = _mla_pre(zc, zm, zm2, mla_q_norm_g[l][None, :], mla_kv_norm_g[l][None, :],
                              wqa, wqb, wk, wv, cosq, sinq, cosk, tt)
        o_c = _mla_attn(qh, kh, vh, tq)

        apad = _place(LANE, [(M_A, jnp.exp(gdn_a_log[l].astype(F32))[None, :])])
        dtpad = _place(LANE, [(M_A, gdn_dt_bias[l][None, :])])
        o_d = _gdn(zd, zm, gdn_conv_w[l], apad, dtpad, ebeta, ea, gdn_norm_g[l][None, :], tt)

        outs = [o.reshape(t, GROUP_WIDTH) for o in (o_a, o_b, o_c, o_d)]
        x = _out_proj(outs, w_out[l].astype(BF16), post_mix_g[l][None, :], x.reshape(t, d), tm).reshape(bsz, s, d)

        x = _ffn(x, pre_ffn_g[l][None, :], ffn_w_gate[l].astype(BF16), ffn_w_up[l].astype(BF16),
                 ffn_conv_w[l], ffn_w_down[l].astype(BF16), post_ffn_g[l][None, :], tm=_pick(s, (512, 256)), fb=fb)
    return x
```

```python
import functools

import jax
import jax.numpy as jnp
from jax import lax
from jax.experimental import pallas as pl
from jax.experimental.pallas import tpu as pltpu

F32 = jnp.float32
BF16 = jnp.bfloat16
HIGHEST = lax.Precision.HIGHEST

N_GROUPS = 4
HEAD_DIM = 64
N_HEADS = 4
GROUP_WIDTH = N_HEADS * HEAD_DIM
EPS = 1e-6
MASK_VALUE = -1e30
MIN_POS = 1e-30
LOG2E = 1.4426950408889634

DK_B = HEAD_DIM // 2
GLA_GATE_RANK = 16
GLA_GATE_NORMALIZER = 16.0
MLA_Q_RANK = 256
MLA_KV_RANK = 128
MLA_NOPE = 64
MLA_ROPE = 32
ROPE_THETA = 10000.0
GDN_CONV = 4
FFN_CONV = 3
LANE = 128
SUBLANE = 8

M_CODE = 0
M_BETA = 16
M_A = 20
M_KR = 64

BAND = 16
GDN_CHUNK = 64
VMEM_LIMIT = 56 * 1024 * 1024


def _cparams(sem):
    return pltpu.CompilerParams(dimension_semantics=sem, vmem_limit_bytes=VMEM_LIMIT)


def _rms(x, g):
    return x * lax.rsqrt(jnp.mean(x * x, axis=-1, keepdims=True) + EPS) * g


def _sigmoid(x):
    return jax.nn.sigmoid(x)


def _softplus(x):
    return jnp.maximum(x, 0.0) + jnp.log1p(jnp.exp(-jnp.abs(x)))


def _block_cumsum(x, rowmod, block):
    s = 1
    while s < block:
        x = x + jnp.where(rowmod >= s, pltpu.roll(x, s, 0), 0.0)
        s *= 2
    return x


def _dot(a, b, dims=(((1,), (0,)), ((), ())), precision=None):
    return lax.dot_general(a, b, dims, precision=precision, preferred_element_type=F32)


NT = (((1,), (1,)), ((), ()))
TN = (((0,), (0,)), ((), ()))


def _in_proj_kernel(x_ref, g_ref, w_ref, *out_refs):
    h = _rms(x_ref[...], g_ref[...]).astype(BF16)
    off = 0
    for o_ref in out_refs:
        n = o_ref.shape[-1]
        o_ref[...] = _dot(h, w_ref[:, off:off + n])
        off += n


def _in_proj(x2, g, w, widths, tm):
    t, d = x2.shape
    n = w.shape[1]
    return pl.pallas_call(
        _in_proj_kernel,
        grid=(t // tm,),
        in_specs=[pl.BlockSpec((tm, d), lambda i: (i, 0)),
                  pl.BlockSpec((1, d), lambda i: (0, 0)),
                  pl.BlockSpec((d, n), lambda i: (0, 0))],
        out_specs=[pl.BlockSpec((tm, wd), lambda i: (i, 0)) for wd in widths],
        out_shape=[jax.ShapeDtypeStruct((t, wd), F32) for wd in widths],
        compiler_params=_cparams(("parallel",)),
        name="in_proj",
    )(x2, g, w)


def _gla_core(q, k, v, lg, s_ref, oacc_ref, kv_ref, sc_ref, dkh):
    tt, dk = q.shape
    dv = v.shape[1]
    half = BAND // 2
    nblk = tt // BAND
    rowmod = lax.broadcasted_iota(jnp.int32, (tt, dk), 0) & (BAND - 1)
    b = _block_cumsum(lg * LOG2E, rowmod, BAND)
    upper = rowmod >= half
    mid = jnp.broadcast_to(b.reshape(nblk, BAND, dk)[:, half - 1:half, :], (nblk, BAND, dk)).reshape(tt, dk)
    bh = jnp.where(upper, b - mid, b)
    rowmod_h = rowmod & (half - 1)

    def roll_half(x, d):
        return pltpu.roll(x.reshape(tt // half, half, x.shape[1]), d, 1).reshape(tt, x.shape[1])

    seg = (lax.broadcasted_iota(jnp.int32, (dk, dv), 0) // dkh
           == lax.broadcasted_iota(jnp.int32, (dk, dv), 1) // HEAD_DIM).astype(BF16)
    def block_kv(n):
        r0 = n * BAND
        bn = b[r0:r0 + BAND]
        kt = (k[r0:r0 + BAND] * jnp.exp2(bn[BAND - 1:BAND] - bn)).astype(BF16)
        kv_ref[n] = _dot(v[r0:r0 + BAND].astype(BF16), kt, TN)

    per_step = -(-nblk // half)
    o = _dot((q * k).astype(BF16), seg) * v
    for n in range(per_step):
        block_kv(n)
    for d in range(1, half):
        kd = roll_half(k, d)
        bd = roll_half(bh, d)
        vd = roll_half(v, d)
        p = q * kd * jnp.exp2(jnp.where(rowmod_h >= d, bh - bd, MASK_VALUE))
        o = o + _dot(p.astype(BF16), seg) * vd
        for n in range(d * per_step, min((d + 1) * per_step, nblk)):
            block_kv(n)

    q_up = (q * jnp.exp2(jnp.where(upper, bh, MASK_VALUE))).astype(BF16)
    k_lo = (k * jnp.exp2(jnp.where(upper, MASK_VALUE, mid - b))).astype(BF16)
    v16 = v.astype(BF16)
    same_blk = (lax.broadcasted_iota(jnp.int32, (tt, tt), 0) // BAND
                == lax.broadcasted_iota(jnp.int32, (tt, tt), 1) // BAND)
    cross = [jnp.where(same_blk, _dot(q_up[:, h * dkh:(h + 1) * dkh], k_lo[:, h * dkh:(h + 1) * dkh], NT), 0.0)
             for h in range(N_HEADS)]
    cross_out = [_dot(a.astype(BF16), v16[:, h * HEAD_DIM:(h + 1) * HEAD_DIM]) for h, a in enumerate(cross)]

    qe = (q * jnp.exp2(b)).astype(BF16)
    for h in range(N_HEADS):
        hs = slice(h * HEAD_DIM, (h + 1) * HEAD_DIM)
        l0 = (h * dkh // LANE) * LANE
        lt = slice(l0, l0 + LANE)
        own = (lax.broadcasted_iota(jnp.int32, (HEAD_DIM, LANE), 1) + l0) // dkh == h
        st = s_ref[hs, lt]
        for n in range(nblk):
            sc_ref[n, hs, lt] = jnp.where(own, st, 0.0).astype(BF16)
            st = st * jnp.exp2(b[n * BAND + BAND - 1:(n + 1) * BAND, lt]) + kv_ref[n, hs, lt]
        s_ref[hs, lt] = st
    for n in range(nblk):
        r0 = n * BAND
        oacc_ref[r0:r0 + BAND, :] = _dot(qe[r0:r0 + BAND], sc_ref[n], NT)
    for h in range(N_HEADS):
        hv = slice(h * HEAD_DIM, (h + 1) * HEAD_DIM)
        oacc_ref[:, hv] = oacc_ref[:, hv] + cross_out[h]
    return o + oacc_ref[...]


def _head_sum(x):
    gw = GROUP_WIDTH
    seg = (lax.broadcasted_iota(jnp.int32, (gw, gw), 0) // HEAD_DIM
           == lax.broadcasted_iota(jnp.int32, (gw, gw), 1) // HEAD_DIM).astype(BF16)
    hi = x.astype(BF16)
    lo = (x - hi.astype(F32)).astype(BF16)
    return _dot(hi, seg) + _dot(lo, seg)


def _head_norm_gate(o, ng, gate_act, o_ref):
    ms = _head_sum(o * o) * (1.0 / HEAD_DIM)
    o_ref[0] = o * lax.rsqrt(ms + EPS) * ng * gate_act


def _hgrn_kernel(z_ref, lb_ref, ng_ref, o_ref, s_ref, oacc_ref, kv_ref, sc_ref):
    @pl.when(pl.program_id(1) == 0)
    def _():
        s_ref[...] = jnp.zeros_like(s_ref)
        sc_ref[...] = jnp.zeros_like(sc_ref)

    gw = GROUP_WIDTH
    q = z_ref[0, :, 0:gw]
    zf = z_ref[0, :, gw:2 * gw]
    v = z_ref[0, :, 2 * gw:3 * gw]
    gate = z_ref[0, :, 3 * gw:4 * gw]
    lb = lb_ref[...]
    f = lb + (1.0 - lb) * _sigmoid(zf)
    lg = jnp.log(jnp.maximum(f, MIN_POS))
    k = (1.0 - lb) * _sigmoid(-zf)
    qh = q * _sigmoid(q) * HEAD_DIM ** -0.5
    o = _gla_core(qh, k, v, lg, s_ref, oacc_ref, kv_ref, sc_ref, HEAD_DIM)
    _head_norm_gate(o, ng_ref[...], _sigmoid(gate), o_ref)


def _gla_kernel(z_ref, m_ref, wgk_ref, bgk_ref, ng_ref, o_ref, s_ref, oacc_ref, kv_ref, sc_ref):
    @pl.when(pl.program_id(1) == 0)
    def _():
        s_ref[...] = jnp.zeros_like(s_ref)
        sc_ref[...] = jnp.zeros_like(sc_ref)

    dk = N_HEADS * DK_B
    q = z_ref[0, :, 0:dk]
    k = z_ref[0, :, dk:2 * dk]
    v = z_ref[0, :, 2 * dk:2 * dk + GROUP_WIDTH]
    gate = z_ref[0, :, 2 * dk + GROUP_WIDTH:2 * dk + 2 * GROUP_WIDTH]
    x = _dot(m_ref[0].astype(BF16), wgk_ref[...]) + bgk_ref[...]
    lg = -_softplus(-x) / GLA_GATE_NORMALIZER
    o = _gla_core(q * DK_B ** -0.5, k, v, lg, s_ref, oacc_ref, kv_ref, sc_ref, DK_B)
    _head_norm_gate(o, ng_ref[...], gate * _sigmoid(gate), o_ref)


def _row(a):
    return pl.BlockSpec(a.shape, lambda b, t: (0,) * a.ndim)


def _gla_scratch(tt, dk):
    nblk = tt // BAND
    return [pltpu.VMEM((GROUP_WIDTH, dk), F32),
            pltpu.VMEM((tt, GROUP_WIDTH), F32),
            pltpu.VMEM((nblk, GROUP_WIDTH, dk), F32),
            pltpu.VMEM((nblk, GROUP_WIDTH, dk), BF16)]


def _hgrn(za, lb, ng, tt):
    bsz, s, _ = za.shape
    return pl.pallas_call(
        _hgrn_kernel,
        grid=(bsz, s // tt),
        in_specs=[pl.BlockSpec((1, tt, za.shape[2]), lambda b, t: (b, t, 0)), _row(lb), _row(ng)],
        out_specs=pl.BlockSpec((1, tt, GROUP_WIDTH), lambda b, t: (b, t, 0)),
        out_shape=jax.ShapeDtypeStruct((bsz, s, GROUP_WIDTH), F32),
        scratch_shapes=_gla_scratch(tt, GROUP_WIDTH),
        compiler_params=_cparams(("parallel", "arbitrary")),
        name="hgrn",
    )(za, lb, ng)


def _gla(zb, zm, wgk, bgk, ng, tt):
    bsz, s, _ = zb.shape
    return pl.pallas_call(
        _gla_kernel,
        grid=(bsz, s // tt),
        in_specs=[pl.BlockSpec((1, tt, zb.shape[2]), lambda b, t: (b, t, 0)),
                  pl.BlockSpec((1, tt, LANE), lambda b, t: (b, t, 0)),
                  _row(wgk), _row(bgk), _row(ng)],
        out_specs=pl.BlockSpec((1, tt, GROUP_WIDTH), lambda b, t: (b, t, 0)),
        out_shape=jax.ShapeDtypeStruct((bsz, s, GROUP_WIDTH), F32),
        scratch_shapes=_gla_scratch(tt, N_HEADS * DK_B),
        compiler_params=_cparams(("parallel", "arbitrary")),
        name="gla",
    )(zb, zm, wgk, bgk, ng)


def _gdn_kernel(z_ref, m_ref, cw_ref, apad_ref, dtpad_ref, ebeta_ref, ea_ref, ng_ref,
                o_ref, xs_ref, s_ref, oacc_ref, u_ref, w_ref, aqk_ref, sc_ref):
    tt = z_ref.shape[1]
    gw = GROUP_WIDTH
    c = GDN_CHUNK
    halo = SUBLANE

    @pl.when(pl.program_id(1) == 0)
    def _():
        s_ref[...] = jnp.zeros_like(s_ref)
        xs_ref[0:halo, :] = jnp.zeros((halo, 3 * gw), F32)

    xs_ref[halo:halo + tt, :] = z_ref[0, :, 0:3 * gw]
    acc = cw_ref[GDN_CONV - 1:GDN_CONV, :] * xs_ref[halo:halo + tt, :]
    for j in range(1, GDN_CONV):
        acc = acc + cw_ref[GDN_CONV - 1 - j:GDN_CONV - j, :] * xs_ref[halo - j:halo - j + tt, :]
    xs_ref[0:halo, :] = xs_ref[tt:tt + halo, :]
    qkv = acc * _sigmoid(acc)

    q_all, k_all, v_all = qkv[:, 0:gw], qkv[:, gw:2 * gw], qkv[:, 2 * gw:3 * gw]

    head_blk = (lax.broadcasted_iota(jnp.int32, (gw, gw), 0) // HEAD_DIM
                == lax.broadcasted_iota(jnp.int32, (gw, gw), 1) // HEAD_DIM)
    def expand(x, e_ref, terms):
        acc, r = None, x
        for _ in range(terms):
            hi = r.astype(BF16)
            r = r - hi.astype(F32)
            part = _dot(hi, e_ref[...])
            acc = part if acc is None else acc + part
        return acc

    qn = q_all * lax.rsqrt(_head_sum(q_all * q_all) + EPS) * HEAD_DIM ** -0.5
    kn = k_all * lax.rsqrt(_head_sum(k_all * k_all) + EPS)

    m = m_ref[0]
    beta_x = expand(_sigmoid(m), ebeta_ref, 2)
    lg_m = -apad_ref[...] * _softplus(m + dtpad_ref[...])
    rowmod_m = lax.broadcasted_iota(jnp.int32, (tt, LANE), 0) & (c - 1)
    b_m = _block_cumsum(lg_m, rowmod_m, c)
    b_x = expand(b_m, ea_ref, 3)
    eb = jnp.exp(b_x)
    kb = kn * beta_x
    vb = v_all * beta_x
    kbe = kb * eb
    qe = qn * eb

    ii = lax.broadcasted_iota(jnp.int32, (tt, tt), 0)
    jj = lax.broadcasted_iota(jnp.int32, (tt, tt), 1)
    same = (ii // c) == (jj // c)
    incl = same & (ii >= jj)
    strict = same & (ii > jj)

    heads = [slice(h * HEAD_DIM, (h + 1) * HEAD_DIM) for h in range(N_HEADS)]
    nms = []
    for h, sl in enumerate(heads):
        bmat = jnp.broadcast_to(b_m[:, M_A + h:M_A + h + 1], (tt, tt))
        lmask = jnp.where(incl, jnp.exp(jnp.minimum(bmat - bmat.T, 0.0)), 0.0)
        kh16 = kn[:, sl].astype(BF16)
        nms.append(jnp.where(strict, -_dot(kb[:, sl].astype(BF16), kh16, NT) * lmask, 0.0))
        aqk_ref[h] = (_dot(qn[:, sl].astype(BF16), kh16, NT) * lmask).astype(BF16)
    tns = list(nms)
    pws = list(nms)
    for _ in range(5):
        pws16 = [p.astype(BF16) for p in pws]
        pws = [_dot(p, p) for p in pws16]
        tns = [t + p + _dot(t.astype(BF16), p.astype(BF16)) for t, p in zip(tns, pws)]
    for sl, tn in zip(heads, tns):
        tn16 = tn.astype(BF16)
        u_ref[:, sl] = vb[:, sl] + _dot(tn16, vb[:, sl].astype(BF16))
        w_ref[:, sl] = kbe[:, sl] + _dot(tn16, kbe[:, sl].astype(BF16))

    decays, pms, rms = [], [], []
    for n in range(tt // c):
        rs = slice(n * c, (n + 1) * c)
        b_last = b_x[n * c + c - 1:(n + 1) * c, :]
        kt_t = (kn[rs] * jnp.exp(b_last - b_x[rs])).T.astype(BF16)
        decays.append(jnp.exp(b_last))
        pms.append(jnp.where(head_blk, _dot(kt_t, w_ref[rs, :].astype(BF16)), 0.0).astype(BF16))
        rms.append(jnp.where(head_blk, _dot(kt_t, u_ref[rs, :].astype(BF16)), 0.0))
    st = s_ref[...]
    for n in range(tt // c):
        st16 = st.astype(BF16)
        sc_ref[n] = st16
        st = st * decays[n] - _dot(pms[n], st16) + rms[n]
    s_ref[...] = st

    for n in range(tt // c):
        rs = slice(n * c, (n + 1) * c)
        lhs = jnp.concatenate([w_ref[rs, :], qe[rs]], axis=0).astype(BF16)
        res = _dot(lhs, sc_ref[n])
        u_ref[rs, :] = u_ref[rs, :] - res[0:c]
        oacc_ref[rs, :] = res[c:2 * c]
    for h in range(N_HEADS):
        sl = slice(h * HEAD_DIM, (h + 1) * HEAD_DIM)
        oacc_ref[:, sl] = oacc_ref[:, sl] + _dot(aqk_ref[h], u_ref[:, sl].astype(BF16))

    zg = z_ref[0, :, 3 * gw:4 * gw]
    _head_norm_gate(oacc_ref[...], ng_ref[...], zg * _sigmoid(zg), o_ref)


def _gdn(zd, zm, cw, apad, dtpad, ebeta, ea, ng, tt):
    bsz, s, _ = zd.shape
    gw = GROUP_WIDTH
    return pl.pallas_call(
        _gdn_kernel,
        grid=(bsz, s // tt),
        in_specs=[pl.BlockSpec((1, tt, zd.shape[2]), lambda b, t: (b, t, 0)),
                  pl.BlockSpec((1, tt, LANE), lambda b, t: (b, t, 0)),
                  _row(cw), _row(apad), _row(dtpad), _row(ebeta), _row(ea), _row(ng)],
        out_specs=pl.BlockSpec((1, tt, gw), lambda b, t: (b, t, 0)),
        out_shape=jax.ShapeDtypeStruct((bsz, s, gw), F32),
        scratch_shapes=[pltpu.VMEM((tt + 2 * SUBLANE, 3 * gw), F32),
                        pltpu.VMEM((gw, gw), F32),
                        pltpu.VMEM((tt, gw), F32),
                        pltpu.VMEM((tt, gw), F32),
                        pltpu.VMEM((tt, gw), F32),
                        pltpu.VMEM((N_HEADS, tt, tt), BF16),
                        pltpu.VMEM((tt // GDN_CHUNK, gw, gw), BF16)],
        compiler_params=_cparams(("parallel", "arbitrary")),
        name="gdn",
    )(zd, zm, cw, apad, dtpad, ebeta, ea, ng)


def _mla_pre_kernel(zc_ref, m1_ref, m2_ref, gq_ref, gkv_ref, wqa_ref, wqb_ref, wk_ref, wv_ref,
                    cosq_ref, sinq_ref, cosk_ref, qt_ref, k_ref, vt_ref):
    scale = (MLA_NOPE + MLA_ROPE) ** -0.5 * LOG2E
    cq = _rms(zc_ref[0, :, 0:MLA_Q_RANK], gq_ref[...]).astype(BF16)
    ckv = _rms(zc_ref[0, :, MLA_Q_RANK:MLA_Q_RANK + MLA_KV_RANK], gkv_ref[...]).astype(BF16)
    qa = _dot(cq, wqa_ref[...])
    qb = _dot(cq, wqb_ref[...])
    kn = _dot(ckv, wk_ref[...])
    vv = _dot(ckv, wv_ref[...])
    cosq, sinq = cosq_ref[...], sinq_ref[...]
    kr = m1_ref[0] * cosk_ref[...] + m2_ref[0] * sinq
    for h in range(N_HEADS):
        sl = slice(h * LANE, (h + 1) * LANE)
        q = (qa[:, sl] * cosq + qb[:, sl] * sinq) * scale
        qt_ref[0, sl, :] = q.T.astype(BF16)
        k_ref[0, h] = (kn[:, sl] + kr).astype(BF16)
    vt_ref[0] = vv.T.astype(BF16)


def _mla_pre(zc, zm, zm2, gq, gkv, wqa, wqb, wk, wv, cosq, sinq, cosk, tm):
    bsz, s, _ = zc.shape
    tok = lambda wd: pl.BlockSpec((1, tm, wd), lambda b, t: (b, t, 0))
    tab = pl.BlockSpec((tm, LANE), lambda b, t: (t, 0))
    tr = lambda rows: pl.BlockSpec((1, rows, tm), lambda b, t: (b, 0, t))
    return pl.pallas_call(
        _mla_pre_kernel,
        grid=(bsz, s // tm),
        in_specs=[tok(zc.shape[2]), tok(LANE), tok(LANE), _row(gq), _row(gkv),
                  _row(wqa), _row(wqb), _row(wk), _row(wv), tab, tab, tab],
        out_specs=[tr(N_HEADS * LANE),
                   pl.BlockSpec((1, N_HEADS, tm, LANE), lambda b, t: (b, 0, t, 0)),
                   tr(GROUP_WIDTH)],
        out_shape=[jax.ShapeDtypeStruct((bsz, N_HEADS * LANE, s), BF16),
                   jax.ShapeDtypeStruct((bsz, N_HEADS, s, LANE), BF16),
                   jax.ShapeDtypeStruct((bsz, GROUP_WIDTH, s), BF16)],
        compiler_params=_cparams(("parallel", "parallel")),
        name="mla_pre",
    )(zc, zm, zm2, gq, gkv, wqa, wqb, wk, wv, cosq, sinq, cosk)


def _mla_attn_kernel(qi_ref, kj_ref, qt_ref, k_ref, vt_ref, o_ref, m_ref, l_ref, acc_ref):
    pair = pl.program_id(1)
    i = qi_ref[pair]
    j = kj_ref[pair]
    tq = qt_ref.shape[2]
    tk = k_ref.shape[2]

    @pl.when(j == 0)
    def _():
        m_ref[...] = jnp.full_like(m_ref, MASK_VALUE)
        l_ref[...] = jnp.zeros_like(l_ref)
        acc_ref[...] = jnp.zeros_like(acc_ref)

    def step(masked):
        scores = [_dot(k_ref[0, h], qt_ref[0, h * LANE:(h + 1) * LANE, :]) for h in range(N_HEADS)]
        if masked:
            keep = (lax.broadcasted_iota(jnp.int32, (tk, tq), 0)
                    <= lax.broadcasted_iota(jnp.int32, (tk, tq), 1))
            scores = [jnp.where(keep, s, MASK_VALUE) for s in scores]
        m_prevs = [m_ref[h:h + 1, :] for h in range(N_HEADS)]
        m_news = [jnp.maximum(mp, jnp.max(s, axis=0, keepdims=True)) for mp, s in zip(m_prevs, scores)]
        ps = [jnp.exp2(s - mn) for s, mn in zip(scores, m_news)]
        for h in range(N_HEADS):
            alpha = jnp.exp2(m_prevs[h] - m_news[h])
            l_ref[h:h + 1, :] = alpha * l_ref[h:h + 1, :] + jnp.sum(ps[h], axis=0, keepdims=True)
            hs = slice(h * HEAD_DIM, (h + 1) * HEAD_DIM)
            acc_ref[hs, :] = alpha * acc_ref[hs, :] + _dot(vt_ref[0, hs, :], ps[h].astype(BF16))
            m_ref[h:h + 1, :] = m_news[h]

    @pl.when(j < i)
    def _():
        step(False)

    @pl.when(j == i)
    def _():
        step(True)
        for h in range(N_HEADS):
            hs = slice(h * HEAD_DIM, (h + 1) * HEAD_DIM)
            acc_ref[hs, :] = acc_ref[hs, :] / l_ref[h:h + 1, :]
        o_ref[0] = acc_ref[...].T


def _mla_attn(qt, k, vt, tq):
    bsz, _, s = qt.shape
    nq = s // tq
    pairs = [(i, j) for i in range(nq) for j in range(i + 1)]
    qi = jnp.array([p[0] for p in pairs], jnp.int32)
    kj = jnp.array([p[1] for p in pairs], jnp.int32)
    grid_spec = pltpu.PrefetchScalarGridSpec(
        num_scalar_prefetch=2,
        grid=(bsz, len(pairs)),
        in_specs=[pl.BlockSpec((1, N_HEADS * LANE, tq), lambda b, p, qi, kj: (b, 0, qi[p])),
                  pl.BlockSpec((1, N_HEADS, tq, LANE), lambda b, p, qi, kj: (b, 0, kj[p], 0)),
                  pl.BlockSpec((1, GROUP_WIDTH, tq), lambda b, p, qi, kj: (b, 0, kj[p]))],
        out_specs=pl.BlockSpec((1, tq, GROUP_WIDTH), lambda b, p, qi, kj: (b, qi[p], 0)),
        scratch_shapes=[pltpu.VMEM((SUBLANE, tq), F32), pltpu.VMEM((SUBLANE, tq), F32),
                        pltpu.VMEM((GROUP_WIDTH, tq), F32)],
    )
    return pl.pallas_call(
        _mla_attn_kernel,
        grid_spec=grid_spec,
        out_shape=jax.ShapeDtypeStruct((bsz, s, GROUP_WIDTH), F32),
        compiler_params=_cparams(("parallel", "arbitrary")),
        name="mla_attn",
    )(qi, kj, qt, k, vt)


def _out_proj_kernel(oa_ref, ob_ref, oc_ref, od_ref, w_ref, g_ref, x_ref, y_ref):
    gw = GROUP_WIDTH
    mix = _dot(oa_ref[...].astype(BF16), w_ref[0:gw, :])
    for n, r in enumerate((ob_ref, oc_ref, od_ref), start=1):
        mix = mix + _dot(r[...].astype(BF16), w_ref[n * gw:(n + 1) * gw, :])
    y_ref[...] = x_ref[...] + _rms(mix, g_ref[...])


def _out_proj(outs, w, g, x2, tm):
    t, d = x2.shape
    grp = pl.BlockSpec((tm, GROUP_WIDTH), lambda i: (i, 0))
    return pl.pallas_call(
        _out_proj_kernel,
        grid=(t // tm,),
        in_specs=[grp, grp, grp, grp,
                  pl.BlockSpec(w.shape, lambda i: (0, 0)),
                  pl.BlockSpec((1, d), lambda i: (0, 0)),
                  pl.BlockSpec((tm, d), lambda i: (i, 0))],
        out_specs=pl.BlockSpec((tm, d), lambda i: (i, 0)),
        out_shape=jax.ShapeDtypeStruct((t, d), F32),
        compiler_params=_cparams(("parallel",)),
        name="out_proj",
    )(*outs, w, g, x2)


def _gelu_tanh(x):
    return 0.5 * x * (1.0 + jnp.tanh(0.7978845608028654 * (x + 0.044715 * (x * x * x))))


def _ffn_kernel(x_ref, g1_ref, wg_ref, wu_ref, cw_ref, wd_ref, g2_ref, y_ref, prev_ref, act_ref, *, fb):
    tm = x_ref.shape[1]
    f = wg_ref.shape[1]

    @pl.when(pl.program_id(1) == 0)
    def _():
        prev_ref[...] = jnp.zeros_like(prev_ref)

    x = x_ref[0]
    h = _rms(x, g1_ref[...]).astype(BF16)
    row = lax.broadcasted_iota(jnp.int32, (tm, fb), 0)
    for n in range(f // fb):
        cs = slice(n * fb, (n + 1) * fb)
        gate = _dot(h, wg_ref[:, cs])
        up = _dot(h, wu_ref[:, cs])
        p1 = prev_ref[1:2, cs]
        p2 = prev_ref[0:1, cs]
        g1 = jnp.where(row == 0, p1, pltpu.roll(gate, 1, 0))
        g2 = jnp.where(row == 0, p2, jnp.where(row == 1, p1, pltpu.roll(gate, 2, 0)))
        prev_ref[0:2, cs] = gate[tm - 2:tm, :]
        conv = cw_ref[2:3, cs] * gate + cw_ref[1:2, cs] * g1 + cw_ref[0:1, cs] * g2
        act_ref[:, cs] = (_gelu_tanh(conv) * up).astype(BF16)
    y_ref[0] = x + _rms(_dot(act_ref[...], wd_ref[...]), g2_ref[...])


def _ffn(x, g1, wg, wu, cw, wd, g2, tm, fb):
    bsz, s, d = x.shape
    once = lambda a: pl.BlockSpec(a.shape, lambda b, t: (0,) * a.ndim, pipeline_mode=pl.Buffered(1))
    return pl.pallas_call(
        functools.partial(_ffn_kernel, fb=fb),
        grid=(bsz, s // tm),
        in_specs=[pl.BlockSpec((1, tm, d), lambda b, t: (b, t, 0)),
                  _row(g1), once(wg), once(wu), _row(cw), once(wd), _row(g2)],
        out_specs=pl.BlockSpec((1, tm, d), lambda b, t: (b, t, 0)),
        out_shape=jax.ShapeDtypeStruct((bsz, s, d), F32),
        scratch_shapes=[pltpu.VMEM((SUBLANE, wg.shape[1]), F32),
                        pltpu.VMEM((tm, wg.shape[1]), BF16)],
        compiler_params=_cparams(("parallel", "arbitrary")),
        name="ffn",
    )(x, g1, wg, wu, cw, wd, g2)


def _place(width, pieces):
    rows = pieces[0][1].shape[0]
    out = jnp.zeros((rows, width), pieces[0][1].dtype)
    for off, a in pieces:
        out = lax.dynamic_update_slice(out, a, (0, off))
    return out


def _rot_cols(w):
    half = w.shape[1] // 2
    return jnp.concatenate([-w[:, half:], w[:, :half]], axis=1)


def _in_proj_layout(w):
    gw, dkb = GROUP_WIDTH, N_HEADS * DK_B
    o_b = 4 * gw
    o_code = o_b + 2 * dkb + gw
    o_bg = o_code + GLA_GATE_RANK
    o_c = o_bg + gw
    o_kr = o_c + MLA_Q_RANK + MLA_KV_RANK
    o_d = o_kr + MLA_ROPE
    o_beta = o_d + 3 * gw
    o_a = o_beta + N_HEADS
    o_z = o_a + N_HEADS
    assert o_z + gw == w.shape[1]
    kr = w[:, o_kr:o_d]
    misc1 = _place(LANE, [(M_CODE, w[:, o_code:o_bg]), (M_BETA, w[:, o_beta:o_a]),
                          (M_A, w[:, o_a:o_z]), (M_KR, kr)])
    misc2 = _place(LANE, [(M_KR, _rot_cols(kr))])
    cols = [w[:, 0:o_b],
            w[:, o_b:o_code], w[:, o_bg:o_c],
            w[:, o_c:o_kr],
            w[:, o_d:o_beta], w[:, o_z:],
            misc1, misc2]
    widths = (4 * gw, 2 * dkb + 2 * gw, MLA_Q_RANK + MLA_KV_RANK, 4 * gw, LANE, LANE)
    return jnp.concatenate(cols, axis=1).astype(BF16), widths


def _mla_layout(w_uq, w_ukv):
    dq = MLA_NOPE + MLA_ROPE
    qa, qb, wk, wv = [], [], [], []
    for h in range(N_HEADS):
        nope = w_uq[:, h * dq:h * dq + MLA_NOPE]
        rope = w_uq[:, h * dq + MLA_NOPE:(h + 1) * dq]
        qa.append(_place(LANE, [(0, nope), (MLA_NOPE, rope)]))
        qb.append(_place(LANE, [(MLA_NOPE, _rot_cols(rope))]))
        kv = w_ukv[:, h * 2 * HEAD_DIM:(h + 1) * 2 * HEAD_DIM]
        wk.append(_place(LANE, [(0, kv[:, :MLA_NOPE])]))
        wv.append(kv[:, MLA_NOPE:])
    cat = lambda xs: jnp.concatenate(xs, axis=1).astype(BF16)
    return cat(qa), cat(qb), cat(wk), cat(wv)


def _rope_tables(s):
    inv = ROPE_THETA ** (-jnp.arange(0, MLA_ROPE, 2, dtype=F32) / MLA_ROPE)
    ang = jnp.arange(s, dtype=F32)[:, None] * inv[None, :]
    cos2 = jnp.concatenate([jnp.cos(ang), jnp.cos(ang)], axis=1)
    sin2 = jnp.concatenate([jnp.sin(ang), jnp.sin(ang)], axis=1)
    ones = jnp.ones((s, MLA_NOPE), F32)
    cosq = _place(LANE, [(0, ones), (MLA_NOPE, cos2)])
    sinq = _place(LANE, [(MLA_NOPE, sin2)])
    cosk = _place(LANE, [(MLA_NOPE, cos2)])
    return cosq, sinq, cosk


def _tile_heads(g):
    return jnp.tile(g, N_HEADS)[None, :]


def _pick(n, prefs):
    for p in prefs:
        if n % p == 0:
            return p
    return n


def kernel(x, w_in, w_out, pre_mix_g, post_mix_g, pre_ffn_g, post_ffn_g, hgrn_lb_logits, hgrn_norm_g,
           gla_w_gk2, gla_b_gk, gla_norm_g, mla_q_norm_g, mla_w_uq, mla_kv_norm_g, mla_w_ukv,
           gdn_conv_w, gdn_a_log, gdn_dt_bias, gdn_norm_g, ffn_w_gate, ffn_w_up, ffn_conv_w, ffn_w_down):
    bsz, s, d = x.shape
    depth = w_in.shape[0]
    t = bsz * s
    tm = _pick(t, (512, 256))
    tt = _pick(s, (256,))
    tq = _pick(s, (1024, 512, 256, 128))
    fb = 256

    cosq, sinq, cosk = _rope_tables(s)
    p = jax.nn.softmax(hgrn_lb_logits.astype(F32), axis=0)
    lower_bounds = jnp.cumsum(p, axis=0) - p[0]

    lanes = jnp.arange(LANE)
    head_of = jnp.arange(GROUP_WIDTH) // HEAD_DIM
    ebeta = (lanes[:, None] == M_BETA + head_of[None, :]).astype(BF16)
    ea = (lanes[:, None] == M_A + head_of[None, :]).astype(BF16)

    for l in range(depth):
        w_in_l, widths = _in_proj_layout(w_in[l])
        za, zb, zc, zd, zm, zm2 = _in_proj(x.reshape(t, d), pre_mix_g[l][None, :], w_in_l, widths, tm)
        r3 = lambda a: a.reshape(bsz, s, a.shape[-1])
        za, zb, zc, zd, zm, zm2 = map(r3, (za, zb, zc, zd, zm, zm2))

        o_a = _hgrn(za, lower_bounds[l][None, :], _tile_heads(hgrn_norm_g[l]), tt)

        wgk = _place(N_HEADS * DK_B, [(0, gla_w_gk2[l])])
        wgk = jnp.concatenate([wgk, jnp.zeros((LANE - GLA_GATE_RANK, N_HEADS * DK_B), F32)], axis=0).astype(BF16)
        o_b = _gla(zb, zm, wgk, gla_b_gk[l][None, :], _tile_heads(gla_norm_g[l]), tt)

        wqa, wqb, wk, wv = _mla_layout(mla_w_uq[l], mla_w_ukv[l])
        qh, kh, vh = _mla_pre(zc, zm, zm2, mla_q_norm_g[l][None, :], mla_kv_norm_g[l][None, :],
                              wqa, wqb, wk, wv, cosq, sinq, cosk, tt)
        o_c = _mla_attn(qh, kh, vh, tq)

        apad = _place(LANE, [(M_A, jnp.exp(gdn_a_log[l].astype(F32))[None, :])])
        dtpad = _place(LANE, [(M_A, gdn_dt_bias[l][None, :])])
        o_d = _gdn(zd, zm, gdn_conv_w[l], apad, dtpad, ebeta, ea, _tile_heads(gdn_norm_g[l]), tt)

        outs = [o.reshape(t, GROUP_WIDTH) for o in (o_a, o_b, o_c, o_d)]
        x = _out_proj(outs, w_out[l].astype(BF16), post_mix_g[l][None, :], x.reshape(t, d), tm).reshape(bsz, s, d)

        x = _ffn(x, pre_ffn_g[l][None, :], ffn_w_gate[l].astype(BF16), ffn_w_up[l].astype(BF16),
                 ffn_conv_w[l], ffn_w_down[l].astype(BF16), post_ffn_g[l][None, :], tm=_pick(s, (512, 256)), fb=fb)
    return x
```

```python
import functools

import jax
import jax.numpy as jnp
from jax import lax
from jax.experimental import pallas as pl
from jax.experimental.pallas import tpu as pltpu

F32 = jnp.float32
BF16 = jnp.bfloat16
HIGHEST = lax.Precision.HIGHEST

N_GROUPS = 4
HEAD_DIM = 64
N_HEADS = 4
GROUP_WIDTH = N_HEADS * HEAD_DIM
EPS = 1e-6
MASK_VALUE = -1e30
MIN_POS = 1e-30
LOG2E = 1.4426950408889634

DK_B = HEAD_DIM // 2
GLA_GATE_RANK = 16
GLA_GATE_NORMALIZER = 16.0
MLA_Q_RANK = 256
MLA_KV_RANK = 128
MLA_NOPE = 64
MLA_ROPE = 32
ROPE_THETA = 10000.0
GDN_CONV = 4
FFN_CONV = 3
LANE = 128
SUBLANE = 8

M_CODE = 0
M_BETA = 16
M_A = 20
M_KR = 64

BAND = 16
GDN_CHUNK = 64
VMEM_LIMIT = 56 * 1024 * 1024


def _cparams(sem):
    return pltpu.CompilerParams(dimension_semantics=sem, vmem_limit_bytes=VMEM_LIMIT)


def _rms(x, g):
    return x * lax.rsqrt(jnp.mean(x * x, axis=-1, keepdims=True) + EPS) * g


def _sigmoid(x):
    return jax.nn.sigmoid(x)


def _softplus(x):
    return jnp.maximum(x, 0.0) + jnp.log1p(jnp.exp(-jnp.abs(x)))


def _block_cumsum(x, rowmod, block):
    s = 1
    while s < block:
        x = x + jnp.where(rowmod >= s, pltpu.roll(x, s, 0), 0.0)
        s *= 2
    return x


def _dot(a, b, dims=(((1,), (0,)), ((), ())), precision=None):
    return lax.dot_general(a, b, dims, precision=precision, preferred_element_type=F32)


NT = (((1,), (1,)), ((), ()))
TN = (((0,), (0,)), ((), ()))


def _in_proj_kernel(x_ref, g_ref, w_ref, *out_refs):
    h = _rms(x_ref[...], g_ref[...]).astype(BF16)
    off = 0
    for o_ref in out_refs:
        n = o_ref.shape[-1]
        o_ref[...] = _dot(h, w_ref[:, off:off + n])
        off += n


def _in_proj(x2, g, w, widths, tm):
    t, d = x2.shape
    n = w.shape[1]
    return pl.pallas_call(
        _in_proj_kernel,
        grid=(t // tm,),
        in_specs=[pl.BlockSpec((tm, d), lambda i: (i, 0)),
                  pl.BlockSpec((1, d), lambda i: (0, 0)),
                  pl.BlockSpec((d, n), lambda i: (0, 0))],
        out_specs=[pl.BlockSpec((tm, wd), lambda i: (i, 0)) for wd in widths],
        out_shape=[jax.ShapeDtypeStruct((t, wd), F32) for wd in widths],
        compiler_params=_cparams(("parallel",)),
        name="in_proj",
    )(x2, g, w)


def _gla_core(q, k, v, lg, s_ref, oacc_ref, kv_ref, sc_ref, dkh):
    tt, dk = q.shape
    dv = v.shape[1]
    half = BAND // 2
    nblk = tt // BAND
    rowmod = lax.broadcasted_iota(jnp.int32, (tt, dk), 0) & (BAND - 1)
    b = _block_cumsum(lg * LOG2E, rowmod, BAND)
    upper = rowmod >= half
    mid = jnp.broadcast_to(b.reshape(nblk, BAND, dk)[:, half - 1:half, :], (nblk, BAND, dk)).reshape(tt, dk)
    bh = jnp.where(upper, b - mid, b)
    rowmod_h = rowmod & (half - 1)

    def roll_half(x, d):
        return pltpu.roll(x.reshape(tt // half, half, x.shape[1]), d, 1).reshape(tt, x.shape[1])

    seg = (lax.broadcasted_iota(jnp.int32, (dk, dv), 0) // dkh
           == lax.broadcasted_iota(jnp.int32, (dk, dv), 1) // HEAD_DIM).astype(BF16)
    def block_kv(n):
        r0 = n * BAND
        bn = b[r0:r0 + BAND]
        kt = (k[r0:r0 + BAND] * jnp.exp2(bn[BAND - 1:BAND] - bn)).astype(BF16)
        kv_ref[n] = _dot(v[r0:r0 + BAND].astype(BF16), kt, TN)

    per_step = -(-nblk // half)
    o = _dot((q * k).astype(BF16), seg) * v
    for n in range(per_step):
        block_kv(n)
    for d in range(1, half):
        kd = roll_half(k, d)
        bd = roll_half(bh, d)
        vd = roll_half(v, d)
        p = q * kd * jnp.exp2(jnp.where(rowmod_h >= d, bh - bd, MASK_VALUE))
        o = o + _dot(p.astype(BF16), seg) * vd
        for n in range(d * per_step, min((d + 1) * per_step, nblk)):
            block_kv(n)

    q_up = (q * jnp.exp2(jnp.where(upper, bh, MASK_VALUE))).astype(BF16)
    k_lo = (k * jnp.exp2(jnp.where(upper, MASK_VALUE, mid - b))).astype(BF16)
    v16 = v.astype(BF16)
    same_blk = (lax.broadcasted_iota(jnp.int32, (tt, tt), 0) // BAND
                == lax.broadcasted_iota(jnp.int32, (tt, tt), 1) // BAND)
    cross = [jnp.where(same_blk, _dot(q_up[:, h * dkh:(h + 1) * dkh], k_lo[:, h * dkh:(h + 1) * dkh], NT), 0.0)
             for h in range(N_HEADS)]
    cross_out = [_dot(a.astype(BF16), v16[:, h * HEAD_DIM:(h + 1) * HEAD_DIM]) for h, a in enumerate(cross)]

    qe = (q * jnp.exp2(b)).astype(BF16)
    for h in range(N_HEADS):
        hs = slice(h * HEAD_DIM, (h + 1) * HEAD_DIM)
        l0 = (h * dkh // LANE) * LANE
        lt = slice(l0, l0 + LANE)
        own = (lax.broadcasted_iota(jnp.int32, (HEAD_DIM, LANE), 1) + l0) // dkh == h
        st = s_ref[hs, lt]
        for n in range(nblk):
            sc_ref[n, hs, lt] = jnp.where(own, st, 0.0).astype(BF16)
            st = st * jnp.exp2(b[n * BAND + BAND - 1:(n + 1) * BAND, lt]) + kv_ref[n, hs, lt]
        s_ref[hs, lt] = st
    for n in range(nblk):
        r0 = n * BAND
        oacc_ref[r0:r0 + BAND, :] = _dot(qe[r0:r0 + BAND], sc_ref[n], NT)
    for h in range(N_HEADS):
        hv = slice(h * HEAD_DIM, (h + 1) * HEAD_DIM)
        oacc_ref[:, hv] = oacc_ref[:, hv] + cross_out[h]
    return o + oacc_ref[...]


def _head_sum(x):
    gw = GROUP_WIDTH
    seg = (lax.broadcasted_iota(jnp.int32, (gw, gw), 0) // HEAD_DIM
           == lax.broadcasted_iota(jnp.int32, (gw, gw), 1) // HEAD_DIM).astype(BF16)
    hi = x.astype(BF16)
    lo = (x - hi.astype(F32)).astype(BF16)
    return _dot(hi, seg) + _dot(lo, seg)


def _head_norm_gate(o, ng, gate_act, o_ref):
    ms = _head_sum(o * o) * (1.0 / HEAD_DIM)
    o_ref[0] = (o * lax.rsqrt(ms + EPS) * ng * gate_act).astype(o_ref.dtype)


def _hgrn_kernel(z_ref, lb_ref, ng_ref, o_ref, s_ref, oacc_ref, kv_ref, sc_ref):
    @pl.when(pl.program_id(1) == 0)
    def _():
        s_ref[...] = jnp.zeros_like(s_ref)
        sc_ref[...] = jnp.zeros_like(sc_ref)

    gw = GROUP_WIDTH
    q = z_ref[0, :, 0:gw]
    zf = z_ref[0, :, gw:2 * gw]
    v = z_ref[0, :, 2 * gw:3 * gw]
    gate = z_ref[0, :, 3 * gw:4 * gw]
    lb = lb_ref[...]
    f = lb + (1.0 - lb) * _sigmoid(zf)
    lg = jnp.log(jnp.maximum(f, MIN_POS))
    k = (1.0 - lb) * _sigmoid(-zf)
    qh = q * _sigmoid(q) * HEAD_DIM ** -0.5
    o = _gla_core(qh, k, v, lg, s_ref, oacc_ref, kv_ref, sc_ref, HEAD_DIM)
    _head_norm_gate(o, ng_ref[...], _sigmoid(gate), o_ref)


def _gla_kernel(z_ref, m_ref, wgk_ref, bgk_ref, ng_ref, o_ref, s_ref, oacc_ref, kv_ref, sc_ref):
    @pl.when(pl.program_id(1) == 0)
    def _():
        s_ref[...] = jnp.zeros_like(s_ref)
        sc_ref[...] = jnp.zeros_like(sc_ref)

    dk = N_HEADS * DK_B
    q = z_ref[0, :, 0:dk]
    k = z_ref[0, :, dk:2 * dk]
    v = z_ref[0, :, 2 * dk:2 * dk + GROUP_WIDTH]
    gate = z_ref[0, :, 2 * dk + GROUP_WIDTH:2 * dk + 2 * GROUP_WIDTH]
    x = _dot(m_ref[0].astype(BF16), wgk_ref[...]) + bgk_ref[...]
    lg = -_softplus(-x) / GLA_GATE_NORMALIZER
    o = _gla_core(q * DK_B ** -0.5, k, v, lg, s_ref, oacc_ref, kv_ref, sc_ref, DK_B)
    _head_norm_gate(o, ng_ref[...], gate * _sigmoid(gate), o_ref)


def _row(a):
    return pl.BlockSpec(a.shape, lambda b, t: (0,) * a.ndim)


def _gla_scratch(tt, dk):
    nblk = tt // BAND
    return [pltpu.VMEM((GROUP_WIDTH, dk), F32),
            pltpu.VMEM((tt, GROUP_WIDTH), F32),
            pltpu.VMEM((nblk, GROUP_WIDTH, dk), F32),
            pltpu.VMEM((nblk, GROUP_WIDTH, dk), BF16)]


def _hgrn(za, lb, ng, tt):
    bsz, s, _ = za.shape
    return pl.pallas_call(
        _hgrn_kernel,
        grid=(bsz, s // tt),
        in_specs=[pl.BlockSpec((1, tt, za.shape[2]), lambda b, t: (b, t, 0)), _row(lb), _row(ng)],
        out_specs=pl.BlockSpec((1, tt, GROUP_WIDTH), lambda b, t: (b, t, 0)),
        out_shape=jax.ShapeDtypeStruct((bsz, s, GROUP_WIDTH), BF16),
        scratch_shapes=_gla_scratch(tt, GROUP_WIDTH),
        compiler_params=_cparams(("parallel", "arbitrary")),
        name="hgrn",
    )(za, lb, ng)


def _gla(zb, zm, wgk, bgk, ng, tt):
    bsz, s, _ = zb.shape
    return pl.pallas_call(
        _gla_kernel,
        grid=(bsz, s // tt),
        in_specs=[pl.BlockSpec((1, tt, zb.shape[2]), lambda b, t: (b, t, 0)),
                  pl.BlockSpec((1, tt, LANE), lambda b, t: (b, t, 0)),
                  _row(wgk), _row(bgk), _row(ng)],
        out_specs=pl.BlockSpec((1, tt, GROUP_WIDTH), lambda b, t: (b, t, 0)),
        out_shape=jax.ShapeDtypeStruct((bsz, s, GROUP_WIDTH), BF16),
        scratch_shapes=_gla_scratch(tt, N_HEADS * DK_B),
        compiler_params=_cparams(("parallel", "arbitrary")),
        name="gla",
    )(zb, zm, wgk, bgk, ng)


def _gdn_kernel(z_ref, m_ref, cw_ref, apad_ref, dtpad_ref, ebeta_ref, ea_ref, ng_ref,
                o_ref, xs_ref, s_ref, oacc_ref, u_ref, w_ref, aqk_ref, sc_ref):
    tt = z_ref.shape[1]
    gw = GROUP_WIDTH
    c = GDN_CHUNK
    halo = SUBLANE

    @pl.when(pl.program_id(1) == 0)
    def _():
        s_ref[...] = jnp.zeros_like(s_ref)
        xs_ref[0:halo, :] = jnp.zeros((halo, 3 * gw), F32)

    xs_ref[halo:halo + tt, :] = z_ref[0, :, 0:3 * gw]
    acc = cw_ref[GDN_CONV - 1:GDN_CONV, :] * xs_ref[halo:halo + tt, :]
    for j in range(1, GDN_CONV):
        acc = acc + cw_ref[GDN_CONV - 1 - j:GDN_CONV - j, :] * xs_ref[halo - j:halo - j + tt, :]
    xs_ref[0:halo, :] = xs_ref[tt:tt + halo, :]
    qkv = acc * _sigmoid(acc)

    q_all, k_all, v_all = qkv[:, 0:gw], qkv[:, gw:2 * gw], qkv[:, 2 * gw:3 * gw]

    head_blk = (lax.broadcasted_iota(jnp.int32, (gw, gw), 0) // HEAD_DIM
                == lax.broadcasted_iota(jnp.int32, (gw, gw), 1) // HEAD_DIM)
    def expand(x, e_ref, terms):
        acc, r = None, x
        for _ in range(terms):
            hi = r.astype(BF16)
            r = r - hi.astype(F32)
            part = _dot(hi, e_ref[...])
            acc = part if acc is None else acc + part
        return acc

    qn = q_all * lax.rsqrt(_head_sum(q_all * q_all) + EPS) * HEAD_DIM ** -0.5
    kn = k_all * lax.rsqrt(_head_sum(k_all * k_all) + EPS)

    m = m_ref[0]
    beta_x = expand(_sigmoid(m), ebeta_ref, 2)
    lg_m = -apad_ref[...] * _softplus(m + dtpad_ref[...])
    rowmod_m = lax.broadcasted_iota(jnp.int32, (tt, LANE), 0) & (c - 1)
    b_m = _block_cumsum(lg_m, rowmod_m, c)
    b_x = expand(b_m, ea_ref, 3)
    eb = jnp.exp(b_x)
    kb = kn * beta_x
    vb = v_all * beta_x
    kbe = kb * eb
    qe = qn * eb

    ii = lax.broadcasted_iota(jnp.int32, (tt, tt), 0)
    jj = lax.broadcasted_iota(jnp.int32, (tt, tt), 1)
    same = (ii // c) == (jj // c)
    incl = same & (ii >= jj)
    strict = same & (ii > jj)

    heads = [slice(h * HEAD_DIM, (h + 1) * HEAD_DIM) for h in range(N_HEADS)]
    nms = []
    for h, sl in enumerate(heads):
        bmat = jnp.broadcast_to(b_m[:, M_A + h:M_A + h + 1], (tt, tt))
        lmask = jnp.where(incl, jnp.exp(jnp.minimum(bmat - bmat.T, 0.0)), 0.0)
        kh16 = kn[:, sl].astype(BF16)
        nms.append(jnp.where(strict, -_dot(kb[:, sl].astype(BF16), kh16, NT) * lmask, 0.0))
        aqk_ref[h] = (_dot(qn[:, sl].astype(BF16), kh16, NT) * lmask).astype(BF16)
    tns = list(nms)
    pws = list(nms)
    for _ in range(5):
        pws16 = [p.astype(BF16) for p in pws]
        pws = [_dot(p, p) for p in pws16]
        tns = [t + p + _dot(t.astype(BF16), p.astype(BF16)) for t, p in zip(tns, pws)]
    for sl, tn in zip(heads, tns):
        tn16 = tn.astype(BF16)
        u_ref[:, sl] = vb[:, sl] + _dot(tn16, vb[:, sl].astype(BF16))
        w_ref[:, sl] = kbe[:, sl] + _dot(tn16, kbe[:, sl].astype(BF16))

    decays, pms, rms = [], [], []
    for n in range(tt // c):
        rs = slice(n * c, (n + 1) * c)
        b_last = b_x[n * c + c - 1:(n + 1) * c, :]
        kt_t = (kn[rs] * jnp.exp(b_last - b_x[rs])).T.astype(BF16)
        decays.append(jnp.exp(b_last))
        pms.append(jnp.where(head_blk, _dot(kt_t, w_ref[rs, :].astype(BF16)), 0.0).astype(BF16))
        rms.append(jnp.where(head_blk, _dot(kt_t, u_ref[rs, :].astype(BF16)), 0.0))
    st = s_ref[...]
    for n in range(tt // c):
        st16 = st.astype(BF16)
        sc_ref[n] = st16
        st = st * decays[n] - _dot(pms[n], st16) + rms[n]
    s_ref[...] = st

    for n in range(tt // c):
        rs = slice(n * c, (n + 1) * c)
        lhs = jnp.concatenate([w_ref[rs, :], qe[rs]], axis=0).astype(BF16)
        res = _dot(lhs, sc_ref[n])
        u_ref[rs, :] = u_ref[rs, :] - res[0:c]
        oacc_ref[rs, :] = res[c:2 * c]
    for h in range(N_HEADS):
        sl = slice(h * HEAD_DIM, (h + 1) * HEAD_DIM)
        oacc_ref[:, sl] = oacc_ref[:, sl] + _dot(aqk_ref[h], u_ref[:, sl].astype(BF16))

    zg = z_ref[0, :, 3 * gw:4 * gw]
    _head_norm_gate(oacc_ref[...], ng_ref[...], zg * _sigmoid(zg), o_ref)


def _gdn(zd, zm, cw, apad, dtpad, ebeta, ea, ng, tt):
    bsz, s, _ = zd.shape
    gw = GROUP_WIDTH
    return pl.pallas_call(
        _gdn_kernel,
        grid=(bsz, s // tt),
        in_specs=[pl.BlockSpec((1, tt, zd.shape[2]), lambda b, t: (b, t, 0)),
                  pl.BlockSpec((1, tt, LANE), lambda b, t: (b, t, 0)),
                  _row(cw), _row(apad), _row(dtpad), _row(ebeta), _row(ea), _row(ng)],
        out_specs=pl.BlockSpec((1, tt, gw), lambda b, t: (b, t, 0)),
        out_shape=jax.ShapeDtypeStruct((bsz, s, gw), BF16),
        scratch_shapes=[pltpu.VMEM((tt + 2 * SUBLANE, 3 * gw), F32),
                        pltpu.VMEM((gw, gw), F32),
                        pltpu.VMEM((tt, gw), F32),
                        pltpu.VMEM((tt, gw), F32),
                        pltpu.VMEM((tt, gw), F32),
                        pltpu.VMEM((N_HEADS, tt, tt), BF16),
                        pltpu.VMEM((tt // GDN_CHUNK, gw, gw), BF16)],
        compiler_params=_cparams(("parallel", "arbitrary")),
        name="gdn",
    )(zd, zm, cw, apad, dtpad, ebeta, ea, ng)


def _mla_pre_kernel(zc_ref, m1_ref, m2_ref, gq_ref, gkv_ref, wqa_ref, wqb_ref, wk_ref, wv_ref,
                    cosq_ref, sinq_ref, cosk_ref, qt_ref, k_ref, vt_ref):
    scale = (MLA_NOPE + MLA_ROPE) ** -0.5 * LOG2E
    cq = _rms(zc_ref[0, :, 0:MLA_Q_RANK], gq_ref[...]).astype(BF16)
    ckv = _rms(zc_ref[0, :, MLA_Q_RANK:MLA_Q_RANK + MLA_KV_RANK], gkv_ref[...]).astype(BF16)
    qa = _dot(cq, wqa_ref[...])
    qb = _dot(cq, wqb_ref[...])
    kn = _dot(ckv, wk_ref[...])
    vv = _dot(ckv, wv_ref[...])
    cosq, sinq = cosq_ref[...], sinq_ref[...]
    kr = m1_ref[0] * cosk_ref[...] + m2_ref[0] * sinq
    for h in range(N_HEADS):
        sl = slice(h * LANE, (h + 1) * LANE)
        q = (qa[:, sl] * cosq + qb[:, sl] * sinq) * scale
        qt_ref[0, sl, :] = q.T.astype(BF16)
        k_ref[0, h] = (kn[:, sl] + kr).astype(BF16)
    vt_ref[0] = vv.T.astype(BF16)


def _mla_pre(zc, zm, zm2, gq, gkv, wqa, wqb, wk, wv, cosq, sinq, cosk, tm):
    bsz, s, _ = zc.shape
    tok = lambda wd: pl.BlockSpec((1, tm, wd), lambda b, t: (b, t, 0))
    tab = pl.BlockSpec((tm, LANE), lambda b, t: (t, 0))
    tr = lambda rows: pl.BlockSpec((1, rows, tm), lambda b, t: (b, 0, t))
    return pl.pallas_call(
        _mla_pre_kernel,
        grid=(bsz, s // tm),
        in_specs=[tok(zc.shape[2]), tok(LANE), tok(LANE), _row(gq), _row(gkv),
                  _row(wqa), _row(wqb), _row(wk), _row(wv), tab, tab, tab],
        out_specs=[tr(N_HEADS * LANE),
                   pl.BlockSpec((1, N_HEADS, tm, LANE), lambda b, t: (b, 0, t, 0)),
                   tr(GROUP_WIDTH)],
        out_shape=[jax.ShapeDtypeStruct((bsz, N_HEADS * LANE, s), BF16),
                   jax.ShapeDtypeStruct((bsz, N_HEADS, s, LANE), BF16),
                   jax.ShapeDtypeStruct((bsz, GROUP_WIDTH, s), BF16)],
        compiler_params=_cparams(("parallel", "parallel")),
        name="mla_pre",
    )(zc, zm, zm2, gq, gkv, wqa, wqb, wk, wv, cosq, sinq, cosk)


def _mla_attn_kernel(qi_ref, kj_ref, qt_ref, k_ref, vt_ref, o_ref, m_ref, l_ref, acc_ref):
    pair = pl.program_id(1)
    i = qi_ref[pair]
    j = kj_ref[pair]
    tq = qt_ref.shape[2]
    tk = k_ref.shape[2]

    @pl.when(j == 0)
    def _():
        m_ref[...] = jnp.full_like(m_ref, MASK_VALUE)
        l_ref[...] = jnp.zeros_like(l_ref)
        acc_ref[...] = jnp.zeros_like(acc_ref)

    def sub_step(k0, q0, masked):
        nk = tk // 2
        ks = slice(k0, k0 + nk)
        qs = slice(q0, tq)
        scores = [_dot(k_ref[0, h, ks, :], qt_ref[0, h * LANE:(h + 1) * LANE, qs]) for h in range(N_HEADS)]
        if masked:
            keep = (lax.broadcasted_iota(jnp.int32, (nk, tq - q0), 0) + k0
                    <= lax.broadcasted_iota(jnp.int32, (nk, tq - q0), 1) + q0)
            scores = [jnp.where(keep, s, MASK_VALUE) for s in scores]
        m_prevs = [m_ref[h:h + 1, qs] for h in range(N_HEADS)]
        m_news = [jnp.maximum(mp, jnp.max(s, axis=0, keepdims=True)) for mp, s in zip(m_prevs, scores)]
        ps = [jnp.exp2(s - mn) for s, mn in zip(scores, m_news)]
        for h in range(N_HEADS):
            alpha = jnp.exp2(m_prevs[h] - m_news[h])
            l_ref[h:h + 1, qs] = alpha * l_ref[h:h + 1, qs] + jnp.sum(ps[h], axis=0, keepdims=True)
            hs = slice(h * HEAD_DIM, (h + 1) * HEAD_DIM)
            acc_ref[hs, qs] = alpha * acc_ref[hs, qs] + _dot(vt_ref[0, hs, ks], ps[h].astype(BF16))
            m_ref[h:h + 1, qs] = m_news[h]

    @pl.when(j < i)
    def _():
        sub_step(0, 0, False)
        sub_step(tk // 2, 0, False)

    @pl.when(j == i)
    def _():
        sub_step(0, 0, True)
        sub_step(tk // 2, tq // 2, True)
        for h in range(N_HEADS):
            hs = slice(h * HEAD_DIM, (h + 1) * HEAD_DIM)
            acc_ref[hs, :] = acc_ref[hs, :] / l_ref[h:h + 1, :]
        o_ref[0] = acc_ref[...].T.astype(o_ref.dtype)


def _mla_attn(qt, k, vt, tq):
    bsz, _, s = qt.shape
    nq = s // tq
    pairs = [(i, j) for i in range(nq) for j in range(i + 1)]
    qi = jnp.array([p[0] for p in pairs], jnp.int32)
    kj = jnp.array([p[1] for p in pairs], jnp.int32)
    grid_spec = pltpu.PrefetchScalarGridSpec(
        num_scalar_prefetch=2,
        grid=(bsz, len(pairs)),
        in_specs=[pl.BlockSpec((1, N_HEADS * LANE, tq), lambda b, p, qi, kj: (b, 0, qi[p])),
                  pl.BlockSpec((1, N_HEADS, tq, LANE), lambda b, p, qi, kj: (b, 0, kj[p], 0)),
                  pl.BlockSpec((1, GROUP_WIDTH, tq), lambda b, p, qi, kj: (b, 0, kj[p]))],
        out_specs=pl.BlockSpec((1, tq, GROUP_WIDTH), lambda b, p, qi, kj: (b, qi[p], 0)),
        scratch_shapes=[pltpu.VMEM((SUBLANE, tq), F32), pltpu.VMEM((SUBLANE, tq), F32),
                        pltpu.VMEM((GROUP_WIDTH, tq), F32)],
    )
    return pl.pallas_call(
        _mla_attn_kernel,
        grid_spec=grid_spec,
        out_shape=jax.ShapeDtypeStruct((bsz, s, GROUP_WIDTH), BF16),
        compiler_params=_cparams(("parallel", "arbitrary")),
        name="mla_attn",
    )(qi, kj, qt, k, vt)


def _out_proj_kernel(oa_ref, ob_ref, oc_ref, od_ref, w_ref, g_ref, x_ref, y_ref):
    gw = GROUP_WIDTH
    mix = _dot(oa_ref[...], w_ref[0:gw, :])
    for n, r in enumerate((ob_ref, oc_ref, od_ref), start=1):
        mix = mix + _dot(r[...], w_ref[n * gw:(n + 1) * gw, :])
    y_ref[...] = x_ref[...] + _rms(mix, g_ref[...])


def _out_proj(outs, w, g, x2, tm):
    t, d = x2.shape
    grp = pl.BlockSpec((tm, GROUP_WIDTH), lambda i: (i, 0))
    return pl.pallas_call(
        _out_proj_kernel,
        grid=(t // tm,),
        in_specs=[grp, grp, grp, grp,
                  pl.BlockSpec(w.shape, lambda i: (0, 0)),
                  pl.BlockSpec((1, d), lambda i: (0, 0)),
                  pl.BlockSpec((tm, d), lambda i: (i, 0))],
        out_specs=pl.BlockSpec((tm, d), lambda i: (i, 0)),
        out_shape=jax.ShapeDtypeStruct((t, d), F32),
        compiler_params=_cparams(("parallel",)),
        name="out_proj",
    )(*outs, w, g, x2)


def _gelu_tanh(x):
    return 0.5 * x * (1.0 + jnp.tanh(0.7978845608028654 * (x + 0.044715 * (x * x * x))))


def _ffn_kernel(x_ref, g1_ref, wg_ref, wu_ref, cw_ref, wd_ref, g2_ref, y_ref, prev_ref, act_ref, *, fb):
    tm = x_ref.shape[1]
    f = wg_ref.shape[1]

    @pl.when(pl.program_id(1) == 0)
    def _():
        prev_ref[...] = jnp.zeros_like(prev_ref)

    x = x_ref[0]
    h = _rms(x, g1_ref[...]).astype(BF16)
    row = lax.broadcasted_iota(jnp.int32, (tm, fb), 0)
    for n in range(f // fb):
        cs = slice(n * fb, (n + 1) * fb)
        gate = _dot(h, wg_ref[:, cs])
        up = _dot(h, wu_ref[:, cs])
        p1 = prev_ref[1:2, cs]
        p2 = prev_ref[0:1, cs]
        g1 = jnp.where(row == 0, p1, pltpu.roll(gate, 1, 0))
        g2 = jnp.where(row == 0, p2, jnp.where(row == 1, p1, pltpu.roll(gate, 2, 0)))
        prev_ref[0:2, cs] = gate[tm - 2:tm, :]
        conv = cw_ref[2:3, cs] * gate + cw_ref[1:2, cs] * g1 + cw_ref[0:1, cs] * g2
        act_ref[:, cs] = (_gelu_tanh(conv) * up).astype(BF16)
    y_ref[0] = x + _rms(_dot(act_ref[...], wd_ref[...]), g2_ref[...])


def _ffn(x, g1, wg, wu, cw, wd, g2, tm, fb):
    bsz, s, d = x.shape
    once = lambda a: pl.BlockSpec(a.shape, lambda b, t: (0,) * a.ndim, pipeline_mode=pl.Buffered(1))
    return pl.pallas_call(
        functools.partial(_ffn_kernel, fb=fb),
        grid=(bsz, s // tm),
        in_specs=[pl.BlockSpec((1, tm, d), lambda b, t: (b, t, 0)),
                  _row(g1), once(wg), once(wu), _row(cw), once(wd), _row(g2)],
        out_specs=pl.BlockSpec((1, tm, d), lambda b, t: (b, t, 0)),
        out_shape=jax.ShapeDtypeStruct((bsz, s, d), F32),
        scratch_shapes=[pltpu.VMEM((SUBLANE, wg.shape[1]), F32),
                        pltpu.VMEM((tm, wg.shape[1]), BF16)],
        compiler_params=_cparams(("parallel", "arbitrary")),
        name="ffn",
    )(x, g1, wg, wu, cw, wd, g2)


def _cat(pieces, like):
    lead = like.shape[:-1]
    return jnp.concatenate([jnp.zeros(lead + (p,), like.dtype) if isinstance(p, int) else p for p in pieces], axis=-1)


def _rot_cols(w):
    half = w.shape[-1] // 2
    return jnp.concatenate([-w[..., half:], w[..., :half]], axis=-1)


def _in_proj_layout(w):
    gw, dkb = GROUP_WIDTH, N_HEADS * DK_B
    o_b = 4 * gw
    o_code = o_b + 2 * dkb + gw
    o_bg = o_code + GLA_GATE_RANK
    o_c = o_bg + gw
    o_kr = o_c + MLA_Q_RANK + MLA_KV_RANK
    o_d = o_kr + MLA_ROPE
    o_beta = o_d + 3 * gw
    o_a = o_beta + N_HEADS
    o_z = o_a + N_HEADS
    assert o_z + gw == w.shape[-1]
    assert (M_CODE, M_BETA, M_A) == (0, GLA_GATE_RANK, GLA_GATE_RANK + N_HEADS)
    kr = w[..., o_kr:o_d]
    misc_gap = M_KR - (M_A + N_HEADS)
    tail_gap = LANE - M_KR - MLA_ROPE
    cols = [w[..., 0:o_b],
            w[..., o_b:o_code], w[..., o_bg:o_c],
            w[..., o_c:o_kr],
            w[..., o_d:o_beta], w[..., o_z:],
            w[..., o_code:o_bg], w[..., o_beta:o_a], w[..., o_a:o_z], misc_gap, kr, tail_gap,
            M_KR, _rot_cols(kr), tail_gap]
    widths = (4 * gw, 2 * dkb + 2 * gw, MLA_Q_RANK + MLA_KV_RANK, 4 * gw, LANE, LANE)
    return _cat(cols, w).astype(BF16), widths


def _mla_layout(w_uq, w_ukv):
    dq = MLA_NOPE + MLA_ROPE
    pad = LANE - dq
    qa, qb, wk, wv = [], [], [], []
    for h in range(N_HEADS):
        nope = w_uq[..., h * dq:h * dq + MLA_NOPE]
        rope = w_uq[..., h * dq + MLA_NOPE:(h + 1) * dq]
        qa += [nope, rope, pad]
        qb += [MLA_NOPE, _rot_cols(rope), pad]
        wk += [w_ukv[..., h * 2 * HEAD_DIM:h * 2 * HEAD_DIM + MLA_NOPE], LANE - MLA_NOPE]
        wv += [w_ukv[..., h * 2 * HEAD_DIM + MLA_NOPE:(h + 1) * 2 * HEAD_DIM]]
    return tuple(_cat(p, like).astype(BF16) for p, like in ((qa, w_uq), (qb, w_uq), (wk, w_ukv), (wv, w_ukv)))


def _rope_tables(s):
    inv = ROPE_THETA ** (-jnp.arange(0, MLA_ROPE, 2, dtype=F32) / MLA_ROPE)
    ang = jnp.arange(s, dtype=F32)[:, None] * inv[None, :]
    cos, sin = jnp.cos(ang), jnp.sin(ang)
    pad = LANE - MLA_NOPE - MLA_ROPE
    cosq = _cat([jnp.ones((s, MLA_NOPE), F32), cos, cos, pad], cos)
    sinq = _cat([MLA_NOPE, sin, sin, pad], cos)
    cosk = _cat([MLA_NOPE, cos, cos, pad], cos)
    return cosq, sinq, cosk


def _tile_heads(g):
    return jnp.tile(g, N_HEADS)[None, :]


def _pick(n, prefs):
    for p in prefs:
        if n % p == 0:
            return p
    return n


def kernel(x, w_in, w_out, pre_mix_g, post_mix_g, pre_ffn_g, post_ffn_g, hgrn_lb_logits, hgrn_norm_g,
           gla_w_gk2, gla_b_gk, gla_norm_g, mla_q_norm_g, mla_w_uq, mla_kv_norm_g, mla_w_ukv,
           gdn_conv_w, gdn_a_log, gdn_dt_bias, gdn_norm_g, ffn_w_gate, ffn_w_up, ffn_conv_w, ffn_w_down):
    bsz, s, d = x.shape
    depth = w_in.shape[0]
    t = bsz * s
    tm = _pick(t, (512, 256))
    tt = _pick(s, (256,))
    ts = _pick(s, (512, 256))
    tq = _pick(s, (1024, 512, 256, 128))
    fb = 256

    cosq, sinq, cosk = _rope_tables(s)
    p = jax.nn.softmax(hgrn_lb_logits.astype(F32), axis=0)
    lower_bounds = jnp.cumsum(p, axis=0) - p[0]

    lanes = jnp.arange(LANE)
    head_of = jnp.arange(GROUP_WIDTH) // HEAD_DIM
    ebeta = (lanes[:, None] == M_BETA + head_of[None, :]).astype(BF16)
    ea = (lanes[:, None] == M_A + head_of[None, :]).astype(BF16)

    w_in_all, widths = _in_proj_layout(w_in)
    wqa, wqb, wk, wv = _mla_layout(mla_w_uq, mla_w_ukv)
    wgk = jnp.concatenate([gla_w_gk2, jnp.zeros((depth, LANE - GLA_GATE_RANK, N_HEADS * DK_B), F32)],
                          axis=1).astype(BF16)
    a_tail = LANE - M_A - N_HEADS
    apad = _cat([M_A, jnp.exp(gdn_a_log.astype(F32)), a_tail], gdn_a_log)
    dtpad = _cat([M_A, gdn_dt_bias, a_tail], gdn_dt_bias)
    w_out16, wg16, wu16, wd16 = (a.astype(BF16) for a in (w_out, ffn_w_gate, ffn_w_up, ffn_w_down))

    for l in range(depth):
        za, zb, zc, zd, zm, zm2 = _in_proj(x.reshape(t, d), pre_mix_g[l][None, :], w_in_all[l], widths, tm)
        r3 = lambda a: a.reshape(bsz, s, a.shape[-1])
        za, zb, zc, zd, zm, zm2 = map(r3, (za, zb, zc, zd, zm, zm2))

        o_a = _hgrn(za, lower_bounds[l][None, :], _tile_heads(hgrn_norm_g[l]), tt)
        o_b = _gla(zb, zm, wgk[l], gla_b_gk[l][None, :], _tile_heads(gla_norm_g[l]), tt)
        qt, kh, vt = _mla_pre(zc, zm, zm2, mla_q_norm_g[l][None, :], mla_kv_norm_g[l][None, :],
                              wqa[l], wqb[l], wk[l], wv[l], cosq, sinq, cosk, ts)
        o_c = _mla_attn(qt, kh, vt, tq)
        o_d = _gdn(zd, zm, gdn_conv_w[l], apad[l][None, :], dtpad[l][None, :], ebeta, ea,
                   _tile_heads(gdn_norm_g[l]), tt)

        outs = [o.reshape(t, GROUP_WIDTH) for o in (o_a, o_b, o_c, o_d)]
        x = _out_proj(outs, w_out16[l], post_mix_g[l][None, :], x.reshape(t, d), tm).reshape(bsz, s, d)
        x = _ffn(x, pre_ffn_g[l][None, :], wg16[l], wu16[l], ffn_conv_w[l], wd16[l], post_ffn_g[l][None, :],
                 tm=ts, fb=fb)
    return x
```

```python
import functools

import jax
import jax.numpy as jnp
from jax import lax
from jax.experimental import pallas as pl
from jax.experimental.pallas import tpu as pltpu

F32 = jnp.float32
BF16 = jnp.bfloat16
HIGHEST = lax.Precision.HIGHEST

N_GROUPS = 4
HEAD_DIM = 64
N_HEADS = 4
GROUP_WIDTH = N_HEADS * HEAD_DIM
EPS = 1e-6
MASK_VALUE = -1e30
MIN_POS = 1e-30
LOG2E = 1.4426950408889634

DK_B = HEAD_DIM // 2
GLA_GATE_RANK = 16
GLA_GATE_NORMALIZER = 16.0
MLA_Q_RANK = 256
MLA_KV_RANK = 128
MLA_NOPE = 64
MLA_ROPE = 32
ROPE_THETA = 10000.0
GDN_CONV = 4
FFN_CONV = 3
LANE = 128
SUBLANE = 8

M_CODE = 0
M_BETA = 16
M_A = 20
M_KR = 64

BAND = 16
GDN_CHUNK = 64
VMEM_LIMIT = 56 * 1024 * 1024


def _cparams(sem):
    return pltpu.CompilerParams(dimension_semantics=sem, vmem_limit_bytes=VMEM_LIMIT)


def _rms(x, g):
    return x * lax.rsqrt(jnp.mean(x * x, axis=-1, keepdims=True) + EPS) * g


def _sigmoid(x):
    return jax.nn.sigmoid(x)


def _softplus(x):
    return jnp.maximum(x, 0.0) + jnp.log1p(jnp.exp(-jnp.abs(x)))


def _block_cumsum(x, rowmod, block):
    s = 1
    while s < block:
        x = x + jnp.where(rowmod >= s, pltpu.roll(x, s, 0), 0.0)
        s *= 2
    return x


def _dot(a, b, dims=(((1,), (0,)), ((), ())), precision=None):
    return lax.dot_general(a, b, dims, precision=precision, preferred_element_type=F32)


NT = (((1,), (1,)), ((), ()))
TN = (((0,), (0,)), ((), ()))


def _in_proj_kernel(x_ref, g_ref, w_ref, *out_refs):
    h = _rms(x_ref[...], g_ref[...]).astype(BF16)
    off = 0
    for o_ref in out_refs:
        n = o_ref.shape[-1]
        o_ref[...] = _dot(h, w_ref[:, off:off + n])
        off += n


def _in_proj(x2, g, w, widths, tm):
    t, d = x2.shape
    n = w.shape[1]
    return pl.pallas_call(
        _in_proj_kernel,
        grid=(t // tm,),
        in_specs=[pl.BlockSpec((tm, d), lambda i: (i, 0)),
                  pl.BlockSpec((1, d), lambda i: (0, 0)),
                  pl.BlockSpec((d, n), lambda i: (0, 0))],
        out_specs=[pl.BlockSpec((tm, wd), lambda i: (i, 0)) for wd in widths],
        out_shape=[jax.ShapeDtypeStruct((t, wd), F32) for wd in widths],
        compiler_params=_cparams(("parallel",)),
        name="in_proj",
    )(x2, g, w)


def _gla_core(q, k, v, lg, s_ref, oacc_ref, kv_ref, sc_ref, dkh):
    tt, dk = q.shape
    dv = v.shape[1]
    half = BAND // 2
    nblk = tt // BAND
    rowmod = lax.broadcasted_iota(jnp.int32, (tt, dk), 0) & (BAND - 1)
    b = _block_cumsum(lg * LOG2E, rowmod, BAND)
    upper = rowmod >= half
    mid = jnp.broadcast_to(b.reshape(nblk, BAND, dk)[:, half - 1:half, :], (nblk, BAND, dk)).reshape(tt, dk)
    bh = jnp.where(upper, b - mid, b)
    rowmod_h = rowmod & (half - 1)
    yield

    def roll_half(x, d):
        return pltpu.roll(x.reshape(tt // half, half, x.shape[1]), d, 1).reshape(tt, x.shape[1])

    seg = (lax.broadcasted_iota(jnp.int32, (dk, dv), 0) // dkh
           == lax.broadcasted_iota(jnp.int32, (dk, dv), 1) // HEAD_DIM).astype(BF16)
    def block_kv(n):
        r0 = n * BAND
        bn = b[r0:r0 + BAND]
        kt = (k[r0:r0 + BAND] * jnp.exp2(bn[BAND - 1:BAND] - bn)).astype(BF16)
        kv_ref[n] = _dot(v[r0:r0 + BAND].astype(BF16), kt, TN)

    qe = (q * jnp.exp2(b)).astype(BF16)
    head_rows = [slice(h * HEAD_DIM, (h + 1) * HEAD_DIM) for h in range(N_HEADS)]
    head_lanes = [slice((h * dkh // LANE) * LANE, (h * dkh // LANE) * LANE + LANE) for h in range(N_HEADS)]
    own = [(lax.broadcasted_iota(jnp.int32, (HEAD_DIM, LANE), 1) + lt.start) // dkh == h
           for h, lt in enumerate(head_lanes)]
    sts = [s_ref[hs, lt] for hs, lt in zip(head_rows, head_lanes)]

    def advance(n):
        r0 = n * BAND
        for h, (hs, lt) in enumerate(zip(head_rows, head_lanes)):
            sc_ref[n, hs, lt] = jnp.where(own[h], sts[h], 0.0).astype(BF16)
            sts[h] = sts[h] * jnp.exp2(b[r0 + BAND - 1:r0 + BAND, lt]) + kv_ref[n, hs, lt]
        oacc_ref[r0:r0 + BAND, :] = _dot(qe[r0:r0 + BAND], sc_ref[n], NT)

    q_up = (q * jnp.exp2(jnp.where(upper, bh, MASK_VALUE))).astype(BF16)
    k_lo = (k * jnp.exp2(jnp.where(upper, MASK_VALUE, mid - b))).astype(BF16)
    v16 = v.astype(BF16)
    same_blk = (lax.broadcasted_iota(jnp.int32, (tt, tt), 0) // BAND
                == lax.broadcasted_iota(jnp.int32, (tt, tt), 1) // BAND)
    cross = [jnp.where(same_blk, _dot(q_up[:, h * dkh:(h + 1) * dkh], k_lo[:, h * dkh:(h + 1) * dkh], NT), 0.0)
             for h in range(N_HEADS)]
    cross_out = [_dot(a.astype(BF16), v16[:, h * HEAD_DIM:(h + 1) * HEAD_DIM]) for h, a in enumerate(cross)]
    yield

    per_step = -(-nblk // half)
    o = _dot((q * k).astype(BF16), seg) * v
    for n in range(per_step):
        block_kv(n)
    for d in range(1, half):
        kd = roll_half(k, d)
        bd = roll_half(bh, d)
        vd = roll_half(v, d)
        p = q * kd * jnp.exp2(jnp.where(rowmod_h >= d, bh - bd, MASK_VALUE))
        o = o + _dot(p.astype(BF16), seg) * vd
        for n in range(d * per_step, min((d + 1) * per_step, nblk)):
            block_kv(n)
        for n in range((d - 1) * per_step, min(d * per_step, nblk)):
            advance(n)
        yield
    for n in range((half - 1) * per_step, nblk):
        advance(n)
    for h, (hs, lt) in enumerate(zip(head_rows, head_lanes)):
        s_ref[hs, lt] = sts[h]

    for h, c_out in enumerate(cross_out):
        hv = slice(h * HEAD_DIM, (h + 1) * HEAD_DIM)
        oacc_ref[:, hv] = oacc_ref[:, hv] + c_out
    return o + oacc_ref[...]


def _head_sum(x):
    gw = GROUP_WIDTH
    seg = (lax.broadcasted_iota(jnp.int32, (gw, gw), 0) // HEAD_DIM
           == lax.broadcasted_iota(jnp.int32, (gw, gw), 1) // HEAD_DIM).astype(BF16)
    hi = x.astype(BF16)
    lo = (x - hi.astype(F32)).astype(BF16)
    return _dot(hi, seg) + _dot(lo, seg)


def _head_norm_gate(o, ng, gate_act, o_ref):
    ms = _head_sum(o * o) * (1.0 / HEAD_DIM)
    o_ref[0] = (o * lax.rsqrt(ms + EPS) * ng * gate_act).astype(o_ref.dtype)


def _hgrn_body(z_ref, lb_ref, ng_ref, o_ref, s_ref, oacc_ref, kv_ref, sc_ref):
    gw = GROUP_WIDTH
    q = z_ref[0, :, 0:gw]
    zf = z_ref[0, :, gw:2 * gw]
    v = z_ref[0, :, 2 * gw:3 * gw]
    gate = z_ref[0, :, 3 * gw:4 * gw]
    lb = lb_ref[...]
    f = lb + (1.0 - lb) * _sigmoid(zf)
    lg = jnp.log(jnp.maximum(f, MIN_POS))
    k = (1.0 - lb) * _sigmoid(-zf)
    qh = q * _sigmoid(q) * HEAD_DIM ** -0.5
    yield
    o = yield from _gla_core(qh, k, v, lg, s_ref, oacc_ref, kv_ref, sc_ref, HEAD_DIM)
    yield
    _head_norm_gate(o, ng_ref[...], _sigmoid(gate), o_ref)


def _gla_body(z_ref, m_ref, wgk_ref, bgk_ref, ng_ref, o_ref, s_ref, oacc_ref, kv_ref, sc_ref):
    dk = N_HEADS * DK_B
    q = z_ref[0, :, 0:dk]
    k = z_ref[0, :, dk:2 * dk]
    v = z_ref[0, :, 2 * dk:2 * dk + GROUP_WIDTH]
    gate = z_ref[0, :, 2 * dk + GROUP_WIDTH:2 * dk + 2 * GROUP_WIDTH]
    x = _dot(m_ref[0].astype(BF16), wgk_ref[...]) + bgk_ref[...]
    lg = -_softplus(-x) / GLA_GATE_NORMALIZER
    yield
    o = yield from _gla_core(q * DK_B ** -0.5, k, v, lg, s_ref, oacc_ref, kv_ref, sc_ref, DK_B)
    yield
    _head_norm_gate(o, ng_ref[...], gate * _sigmoid(gate), o_ref)


def _row(a):
    return pl.BlockSpec(a.shape, lambda b, t: (0,) * a.ndim)


def _gla_scratch(tt, dk):
    nblk = tt // BAND
    return [pltpu.VMEM((GROUP_WIDTH, dk), F32),
            pltpu.VMEM((tt, GROUP_WIDTH), F32),
            pltpu.VMEM((nblk, GROUP_WIDTH, dk), F32),
            pltpu.VMEM((nblk, GROUP_WIDTH, dk), BF16)]


def _gdn_body(z_ref, m_ref, cw_ref, apad_ref, dtpad_ref, ebeta_ref, ea_ref, ng_ref,
              o_ref, xs_ref, s_ref, oacc_ref, u_ref, w_ref, aqk_ref, sc_ref):
    tt = z_ref.shape[1]
    gw = GROUP_WIDTH
    c = GDN_CHUNK
    halo = SUBLANE

    xs_ref[halo:halo + tt, :] = z_ref[0, :, 0:3 * gw]
    acc = cw_ref[GDN_CONV - 1:GDN_CONV, :] * xs_ref[halo:halo + tt, :]
    for j in range(1, GDN_CONV):
        acc = acc + cw_ref[GDN_CONV - 1 - j:GDN_CONV - j, :] * xs_ref[halo - j:halo - j + tt, :]
    xs_ref[0:halo, :] = xs_ref[tt:tt + halo, :]
    qkv = acc * _sigmoid(acc)
    yield

    q_all, k_all, v_all = qkv[:, 0:gw], qkv[:, gw:2 * gw], qkv[:, 2 * gw:3 * gw]

    head_blk = (lax.broadcasted_iota(jnp.int32, (gw, gw), 0) // HEAD_DIM
                == lax.broadcasted_iota(jnp.int32, (gw, gw), 1) // HEAD_DIM)
    def expand(x, e_ref, terms):
        acc, r = None, x
        for _ in range(terms):
            hi = r.astype(BF16)
            r = r - hi.astype(F32)
            part = _dot(hi, e_ref[...])
            acc = part if acc is None else acc + part
        return acc

    qn = q_all * lax.rsqrt(_head_sum(q_all * q_all) + EPS) * HEAD_DIM ** -0.5
    kn = k_all * lax.rsqrt(_head_sum(k_all * k_all) + EPS)
    yield

    m = m_ref[0]
    beta_x = expand(_sigmoid(m), ebeta_ref, 2)
    lg_m = -apad_ref[...] * _softplus(m + dtpad_ref[...])
    rowmod_m = lax.broadcasted_iota(jnp.int32, (tt, LANE), 0) & (c - 1)
    b_m = _block_cumsum(lg_m, rowmod_m, c)
    b_x = expand(b_m, ea_ref, 3)
    eb = jnp.exp(b_x)
    kb = kn * beta_x
    vb = v_all * beta_x
    kbe = kb * eb
    qe = qn * eb
    yield

    ii = lax.broadcasted_iota(jnp.int32, (tt, tt), 0)
    jj = lax.broadcasted_iota(jnp.int32, (tt, tt), 1)
    same = (ii // c) == (jj // c)
    incl = same & (ii >= jj)
    strict = same & (ii > jj)

    heads = [slice(h * HEAD_DIM, (h + 1) * HEAD_DIM) for h in range(N_HEADS)]
    nms = []
    for h, sl in enumerate(heads):
        bmat = jnp.broadcast_to(b_m[:, M_A + h:M_A + h + 1], (tt, tt))
        lmask = jnp.where(incl, jnp.exp(jnp.minimum(bmat - bmat.T, 0.0)), 0.0)
        kh16 = kn[:, sl].astype(BF16)
        nms.append(jnp.where(strict, -_dot(kb[:, sl].astype(BF16), kh16, NT) * lmask, 0.0))
        aqk_ref[h] = (_dot(qn[:, sl].astype(BF16), kh16, NT) * lmask).astype(BF16)
        if h % 2 == 1:
            yield
    tns = list(nms)
    pws = list(nms)
    for _ in range(5):
        pws16 = [p.astype(BF16) for p in pws]
        pws = [_dot(p, p) for p in pws16]
        tns = [t + p + _dot(t.astype(BF16), p.astype(BF16)) for t, p in zip(tns, pws)]
        yield
    for sl, tn in zip(heads, tns):
        tn16 = tn.astype(BF16)
        u_ref[:, sl] = vb[:, sl] + _dot(tn16, vb[:, sl].astype(BF16))
        w_ref[:, sl] = kbe[:, sl] + _dot(tn16, kbe[:, sl].astype(BF16))
    yield

    decays, pms, rms = [], [], []
    for n in range(tt // c):
        rs = slice(n * c, (n + 1) * c)
        b_last = b_x[n * c + c - 1:(n + 1) * c, :]
        kt_t = (kn[rs] * jnp.exp(b_last - b_x[rs])).T.astype(BF16)
        decays.append(jnp.exp(b_last))
        pms.append(jnp.where(head_blk, _dot(kt_t, w_ref[rs, :].astype(BF16)), 0.0).astype(BF16))
        rms.append(jnp.where(head_blk, _dot(kt_t, u_ref[rs, :].astype(BF16)), 0.0))
    yield
    st = s_ref[...]
    for n in range(tt // c):
        st16 = st.astype(BF16)
        sc_ref[n] = st16
        st = st * decays[n] - _dot(pms[n], st16) + rms[n]
    s_ref[...] = st
    yield

    for n in range(tt // c):
        rs = slice(n * c, (n + 1) * c)
        lhs = jnp.concatenate([w_ref[rs, :], qe[rs]], axis=0).astype(BF16)
        res = _dot(lhs, sc_ref[n])
        u_ref[rs, :] = u_ref[rs, :] - res[0:c]
        oacc_ref[rs, :] = res[c:2 * c]
    for h in range(N_HEADS):
        sl = slice(h * HEAD_DIM, (h + 1) * HEAD_DIM)
        oacc_ref[:, sl] = oacc_ref[:, sl] + _dot(aqk_ref[h], u_ref[:, sl].astype(BF16))
    yield

    zg = z_ref[0, :, 3 * gw:4 * gw]
    _head_norm_gate(oacc_ref[...], ng_ref[...], zg * _sigmoid(zg), o_ref)


def _interleave(*stages):
    live = list(stages)
    while live:
        for g in list(live):
            try:
                next(g)
            except StopIteration:
                live.remove(g)


N_HGRN_IN, N_GLA_IN, N_GDN_IN = 3, 5, 8
N_GLA_SCRATCH, N_GDN_SCRATCH = 4, 7
GDN_LEAD_STAGES = 6


def _mixers_kernel(*refs):
    it = iter(refs)
    take = lambda n: [next(it) for _ in range(n)]
    hgrn_in, gla_in, gdn_in = take(N_HGRN_IN), take(N_GLA_IN), take(N_GDN_IN)
    oa_ref, ob_ref, od_ref = take(3)
    hgrn_scr, gla_scr, gdn_scr = take(N_GLA_SCRATCH), take(N_GLA_SCRATCH), take(N_GDN_SCRATCH)

    @pl.when(pl.program_id(1) == 0)
    def _():
        for scr in (hgrn_scr, gla_scr):
            scr[0][...] = jnp.zeros_like(scr[0])
            scr[3][...] = jnp.zeros_like(scr[3])
        gdn_scr[1][...] = jnp.zeros_like(gdn_scr[1])
        gdn_scr[0][0:SUBLANE, :] = jnp.zeros((SUBLANE, gdn_scr[0].shape[1]), F32)

    gdn = _gdn_body(*gdn_in, od_ref, *gdn_scr)
    for _ in range(GDN_LEAD_STAGES):
        next(gdn)
    _interleave(gdn,
                _hgrn_body(*hgrn_in, oa_ref, *hgrn_scr),
                _gla_body(*gla_in, ob_ref, *gla_scr))


def _mixers(za, zb, zd, zm, hgrn_params, gla_params, gdn_params, tt):
    bsz, s, _ = za.shape
    gw = GROUP_WIDTH
    tok = lambda a: pl.BlockSpec((1, tt, a.shape[2]), lambda b, t: (b, t, 0))
    operands = [za, *hgrn_params, zb, zm, *gla_params, zd, zm, *gdn_params]
    in_specs = ([tok(za)] + [_row(a) for a in hgrn_params]
                + [tok(zb), tok(zm)] + [_row(a) for a in gla_params]
                + [tok(zd), tok(zm)] + [_row(a) for a in gdn_params])
    assert (len(hgrn_params) + 1, len(gla_params) + 2, len(gdn_params) + 2) == (N_HGRN_IN, N_GLA_IN, N_GDN_IN)
    out = pl.BlockSpec((1, tt, gw), lambda b, t: (b, t, 0))
    gdn_scratch = [pltpu.VMEM((tt + 2 * SUBLANE, 3 * gw), F32),
                   pltpu.VMEM((gw, gw), F32),
                   pltpu.VMEM((tt, gw), F32),
                   pltpu.VMEM((tt, gw), F32),
                   pltpu.VMEM((tt, gw), F32),
                   pltpu.VMEM((N_HEADS, tt, tt), BF16),
                   pltpu.VMEM((tt // GDN_CHUNK, gw, gw), BF16)]
    assert len(gdn_scratch) == N_GDN_SCRATCH and len(_gla_scratch(tt, gw)) == N_GLA_SCRATCH
    return pl.pallas_call(
        _mixers_kernel,
        grid=(bsz, s // tt),
        in_specs=in_specs,
        out_specs=[out, out, out],
        out_shape=[jax.ShapeDtypeStruct((bsz, s, gw), BF16)] * 3,
        scratch_shapes=_gla_scratch(tt, gw) + _gla_scratch(tt, N_HEADS * DK_B) + gdn_scratch,
        compiler_params=_cparams(("parallel", "arbitrary")),
        name="mixers",
    )(*operands)


def _mla_pre_kernel(zc_ref, m1_ref, m2_ref, gq_ref, gkv_ref, wqa_ref, wqb_ref, wk_ref, wv_ref,
                    cosq_ref, sinq_ref, cosk_ref, qt_ref, k_ref, vt_ref):
    scale = (MLA_NOPE + MLA_ROPE) ** -0.5 * LOG2E
    cq = _rms(zc_ref[0, :, 0:MLA_Q_RANK], gq_ref[...]).astype(BF16)
    ckv = _rms(zc_ref[0, :, MLA_Q_RANK:MLA_Q_RANK + MLA_KV_RANK], gkv_ref[...]).astype(BF16)
    qa = _dot(cq, wqa_ref[...])
    qb = _dot(cq, wqb_ref[...])
    kn = _dot(ckv, wk_ref[...])
    vv = _dot(ckv, wv_ref[...])
    cosq, sinq = cosq_ref[...], sinq_ref[...]
    kr = m1_ref[0] * cosk_ref[...] + m2_ref[0] * sinq
    for h in range(N_HEADS):
        sl = slice(h * LANE, (h + 1) * LANE)
        q = (qa[:, sl] * cosq + qb[:, sl] * sinq) * scale
        qt_ref[0, sl, :] = q.T.astype(BF16)
        k_ref[0, h] = (kn[:, sl] + kr).astype(BF16)
    vt_ref[0] = vv.T.astype(BF16)


def _mla_pre(zc, zm, zm2, gq, gkv, wqa, wqb, wk, wv, cosq, sinq, cosk, tm):
    bsz, s, _ = zc.shape
    tok = lambda wd: pl.BlockSpec((1, tm, wd), lambda b, t: (b, t, 0))
    tab = pl.BlockSpec((tm, LANE), lambda b, t: (t, 0))
    tr = lambda rows: pl.BlockSpec((1, rows, tm), lambda b, t: (b, 0, t))
    return pl.pallas_call(
        _mla_pre_kernel,
        grid=(bsz, s // tm),
        in_specs=[tok(zc.shape[2]), tok(LANE), tok(LANE), _row(gq), _row(gkv),
                  _row(wqa), _row(wqb), _row(wk), _row(wv), tab, tab, tab],
        out_specs=[tr(N_HEADS * LANE),
                   pl.BlockSpec((1, N_HEADS, tm, LANE), lambda b, t: (b, 0, t, 0)),
                   tr(GROUP_WIDTH)],
        out_shape=[jax.ShapeDtypeStruct((bsz, N_HEADS * LANE, s), BF16),
                   jax.ShapeDtypeStruct((bsz, N_HEADS, s, LANE), BF16),
                   jax.ShapeDtypeStruct((bsz, GROUP_WIDTH, s), BF16)],
        compiler_params=_cparams(("parallel", "parallel")),
        name="mla_pre",
    )(zc, zm, zm2, gq, gkv, wqa, wqb, wk, wv, cosq, sinq, cosk)


def _mla_attn_kernel(qi_ref, kj_ref, qt_ref, k_ref, vt_ref, o_ref, m_ref, l_ref, acc_ref):
    pair = pl.program_id(1)
    i = qi_ref[pair]
    j = kj_ref[pair]
    tq = qt_ref.shape[2]
    tk = k_ref.shape[2]

    @pl.when(j == 0)
    def _():
        m_ref[...] = jnp.full_like(m_ref, MASK_VALUE)
        l_ref[...] = jnp.zeros_like(l_ref)
        acc_ref[...] = jnp.zeros_like(acc_ref)

    def sub_step(k0, q0, masked):
        nk = tk // 2
        ks = slice(k0, k0 + nk)
        qs = slice(q0, tq)
        scores = [_dot(k_ref[0, h, ks, :], qt_ref[0, h * LANE:(h + 1) * LANE, qs]) for h in range(N_HEADS)]
        if masked:
            keep = (lax.broadcasted_iota(jnp.int32, (nk, tq - q0), 0) + k0
                    <= lax.broadcasted_iota(jnp.int32, (nk, tq - q0), 1) + q0)
            scores = [jnp.where(keep, s, MASK_VALUE) for s in scores]
        m_prevs = [m_ref[h:h + 1, qs] for h in range(N_HEADS)]
        m_news = [jnp.maximum(mp, jnp.max(s, axis=0, keepdims=True)) for mp, s in zip(m_prevs, scores)]
        ps = [jnp.exp2(s - mn) for s, mn in zip(scores, m_news)]
        for h in range(N_HEADS):
            alpha = jnp.exp2(m_prevs[h] - m_news[h])
            l_ref[h:h + 1, qs] = alpha * l_ref[h:h + 1, qs] + jnp.sum(ps[h], axis=0, keepdims=True)
            hs = slice(h * HEAD_DIM, (h + 1) * HEAD_DIM)
            acc_ref[hs, qs] = alpha * acc_ref[hs, qs] + _dot(vt_ref[0, hs, ks], ps[h].astype(BF16))
            m_ref[h:h + 1, qs] = m_news[h]

    @pl.when(j < i)
    def _():
        sub_step(0, 0, False)
        sub_step(tk // 2, 0, False)

    @pl.when(j == i)
    def _():
        sub_step(0, 0, True)
        sub_step(tk // 2, tq // 2, True)
        for h in range(N_HEADS):
            hs = slice(h * HEAD_DIM, (h + 1) * HEAD_DIM)
            acc_ref[hs, :] = acc_ref[hs, :] / l_ref[h:h + 1, :]
        o_ref[0] = acc_ref[...].T.astype(o_ref.dtype)


def _mla_attn(qt, k, vt, tq):
    bsz, _, s = qt.shape
    nq = s // tq
    pairs = [(i, j) for i in range(nq) for j in range(i + 1)]
    qi = jnp.array([p[0] for p in pairs], jnp.int32)
    kj = jnp.array([p[1] for p in pairs], jnp.int32)
    grid_spec = pltpu.PrefetchScalarGridSpec(
        num_scalar_prefetch=2,
        grid=(bsz, len(pairs)),
        in_specs=[pl.BlockSpec((1, N_HEADS * LANE, tq), lambda b, p, qi, kj: (b, 0, qi[p])),
                  pl.BlockSpec((1, N_HEADS, tq, LANE), lambda b, p, qi, kj: (b, 0, kj[p], 0)),
                  pl.BlockSpec((1, GROUP_WIDTH, tq), lambda b, p, qi, kj: (b, 0, kj[p]))],
        out_specs=pl.BlockSpec((1, tq, GROUP_WIDTH), lambda b, p, qi, kj: (b, qi[p], 0)),
        scratch_shapes=[pltpu.VMEM((SUBLANE, tq), F32), pltpu.VMEM((SUBLANE, tq), F32),
                        pltpu.VMEM((GROUP_WIDTH, tq), F32)],
    )
    return pl.pallas_call(
        _mla_attn_kernel,
        grid_spec=grid_spec,
        out_shape=jax.ShapeDtypeStruct((bsz, s, GROUP_WIDTH), BF16),
        compiler_params=_cparams(("parallel", "arbitrary")),
        name="mla_attn",
    )(qi, kj, qt, k, vt)


def _out_proj_kernel(oa_ref, ob_ref, oc_ref, od_ref, w_ref, g_ref, x_ref, y_ref):
    gw = GROUP_WIDTH
    mix = _dot(oa_ref[...], w_ref[0:gw, :])
    for n, r in enumerate((ob_ref, oc_ref, od_ref), start=1):
        mix = mix + _dot(r[...], w_ref[n * gw:(n + 1) * gw, :])
    y_ref[...] = x_ref[...] + _rms(mix, g_ref[...])


def _out_proj(outs, w, g, x2, tm):
    t, d = x2.shape
    grp = pl.BlockSpec((tm, GROUP_WIDTH), lambda i: (i, 0))
    return pl.pallas_call(
        _out_proj_kernel,
        grid=(t // tm,),
        in_specs=[grp, grp, grp, grp,
                  pl.BlockSpec(w.shape, lambda i: (0, 0)),
                  pl.BlockSpec((1, d), lambda i: (0, 0)),
                  pl.BlockSpec((tm, d), lambda i: (i, 0))],
        out_specs=pl.BlockSpec((tm, d), lambda i: (i, 0)),
        out_shape=jax.ShapeDtypeStruct((t, d), F32),
        compiler_params=_cparams(("parallel",)),
        name="out_proj",
    )(*outs, w, g, x2)


def _gelu_tanh(x):
    return 0.5 * x * (1.0 + jnp.tanh(0.7978845608028654 * (x + 0.044715 * (x * x * x))))


def _ffn_kernel(x_ref, g1_ref, wg_ref, wu_ref, cw_ref, wd_ref, g2_ref, y_ref, prev_ref, act_ref, *, fb):
    tm = x_ref.shape[1]
    f = wg_ref.shape[1]

    @pl.when(pl.program_id(1) == 0)
    def _():
        prev_ref[...] = jnp.zeros_like(prev_ref)

    x = x_ref[0]
    h = _rms(x, g1_ref[...]).astype(BF16)
    row = lax.broadcasted_iota(jnp.int32, (tm, fb), 0)
    for n in range(f // fb):
        cs = slice(n * fb, (n + 1) * fb)
        gate = _dot(h, wg_ref[:, cs])
        up = _dot(h, wu_ref[:, cs])
        p1 = prev_ref[1:2, cs]
        p2 = prev_ref[0:1, cs]
        g1 = jnp.where(row == 0, p1, pltpu.roll(gate, 1, 0))
        g2 = jnp.where(row == 0, p2, jnp.where(row == 1, p1, pltpu.roll(gate, 2, 0)))
        prev_ref[0:2, cs] = gate[tm - 2:tm, :]
        conv = cw_ref[2:3, cs] * gate + cw_ref[1:2, cs] * g1 + cw_ref[0:1, cs] * g2
        act_ref[:, cs] = (_gelu_tanh(conv) * up).astype(BF16)
    y_ref[0] = x + _rms(_dot(act_ref[...], wd_ref[...]), g2_ref[...])


def _ffn(x, g1, wg, wu, cw, wd, g2, tm, fb):
    bsz, s, d = x.shape
    once = lambda a: pl.BlockSpec(a.shape, lambda b, t: (0,) * a.ndim, pipeline_mode=pl.Buffered(1))
    return pl.pallas_call(
        functools.partial(_ffn_kernel, fb=fb),
        grid=(bsz, s // tm),
        in_specs=[pl.BlockSpec((1, tm, d), lambda b, t: (b, t, 0)),
                  _row(g1), once(wg), once(wu), _row(cw), once(wd), _row(g2)],
        out_specs=pl.BlockSpec((1, tm, d), lambda b, t: (b, t, 0)),
        out_shape=jax.ShapeDtypeStruct((bsz, s, d), F32),
        scratch_shapes=[pltpu.VMEM((SUBLANE, wg.shape[1]), F32),
                        pltpu.VMEM((tm, wg.shape[1]), BF16)],
        compiler_params=_cparams(("parallel", "arbitrary")),
        name="ffn",
    )(x, g1, wg, wu, cw, wd, g2)


def _cat(pieces, like):
    lead = like.shape[:-1]
    return jnp.concatenate([jnp.zeros(lead + (p,), like.dtype) if isinstance(p, int) else p for p in pieces], axis=-1)


def _rot_cols(w):
    half = w.shape[-1] // 2
    return jnp.concatenate([-w[..., half:], w[..., :half]], axis=-1)


def _in_proj_layout(w):
    gw, dkb = GROUP_WIDTH, N_HEADS * DK_B
    o_b = 4 * gw
    o_code = o_b + 2 * dkb + gw
    o_bg = o_code + GLA_GATE_RANK
    o_c = o_bg + gw
    o_kr = o_c + MLA_Q_RANK + MLA_KV_RANK
    o_d = o_kr + MLA_ROPE
    o_beta = o_d + 3 * gw
    o_a = o_beta + N_HEADS
    o_z = o_a + N_HEADS
    assert o_z + gw == w.shape[-1]
    assert (M_CODE, M_BETA, M_A) == (0, GLA_GATE_RANK, GLA_GATE_RANK + N_HEADS)
    kr = w[..., o_kr:o_d]
    misc_gap = M_KR - (M_A + N_HEADS)
    tail_gap = LANE - M_KR - MLA_ROPE
    cols = [w[..., 0:o_b],
            w[..., o_b:o_code], w[..., o_bg:o_c],
            w[..., o_c:o_kr],
            w[..., o_d:o_beta], w[..., o_z:],
            w[..., o_code:o_bg], w[..., o_beta:o_a], w[..., o_a:o_z], misc_gap, kr, tail_gap,
            M_KR, _rot_cols(kr), tail_gap]
    widths = (4 * gw, 2 * dkb + 2 * gw, MLA_Q_RANK + MLA_KV_RANK, 4 * gw, LANE, LANE)
    return _cat(cols, w).astype(BF16), widths


def _mla_layout(w_uq, w_ukv):
    dq = MLA_NOPE + MLA_ROPE
    pad = LANE - dq
    qa, qb, wk, wv = [], [], [], []
    for h in range(N_HEADS):
        nope = w_uq[..., h * dq:h * dq + MLA_NOPE]
        rope = w_uq[..., h * dq + MLA_NOPE:(h + 1) * dq]
        qa += [nope, rope, pad]
        qb += [MLA_NOPE, _rot_cols(rope), pad]
        wk += [w_ukv[..., h * 2 * HEAD_DIM:h * 2 * HEAD_DIM + MLA_NOPE], LANE - MLA_NOPE]
        wv += [w_ukv[..., h * 2 * HEAD_DIM + MLA_NOPE:(h + 1) * 2 * HEAD_DIM]]
    return tuple(_cat(p, like).astype(BF16) for p, like in ((qa, w_uq), (qb, w_uq), (wk, w_ukv), (wv, w_ukv)))


def _rope_tables(s):
    inv = ROPE_THETA ** (-jnp.arange(0, MLA_ROPE, 2, dtype=F32) / MLA_ROPE)
    ang = jnp.arange(s, dtype=F32)[:, None] * inv[None, :]
    cos, sin = jnp.cos(ang), jnp.sin(ang)
    pad = LANE - MLA_NOPE - MLA_ROPE
    cosq = _cat([jnp.ones((s, MLA_NOPE), F32), cos, cos, pad], cos)
    sinq = _cat([MLA_NOPE, sin, sin, pad], cos)
    cosk = _cat([MLA_NOPE, cos, cos, pad], cos)
    return cosq, sinq, cosk


def _tile_heads(g):
    return jnp.tile(g, N_HEADS)[None, :]


def _pick(n, prefs):
    for p in prefs:
        if n % p == 0:
            return p
    return n


def kernel(x, w_in, w_out, pre_mix_g, post_mix_g, pre_ffn_g, post_ffn_g, hgrn_lb_logits, hgrn_norm_g,
           gla_w_gk2, gla_b_gk, gla_norm_g, mla_q_norm_g, mla_w_uq, mla_kv_norm_g, mla_w_ukv,
           gdn_conv_w, gdn_a_log, gdn_dt_bias, gdn_norm_g, ffn_w_gate, ffn_w_up, ffn_conv_w, ffn_w_down):
    bsz, s, d = x.shape
    depth = w_in.shape[0]
    t = bsz * s
    tm = _pick(t, (512, 256))
    tt = _pick(s, (256,))
    ts = _pick(s, (512, 256))
    tq = _pick(s, (1024, 512, 256, 128))
    fb = 256

    cosq, sinq, cosk = _rope_tables(s)
    p = jax.nn.softmax(hgrn_lb_logits.astype(F32), axis=0)
    lower_bounds = jnp.cumsum(p, axis=0) - p[0]

    lanes = jnp.arange(LANE)
    head_of = jnp.arange(GROUP_WIDTH) // HEAD_DIM
    ebeta = (lanes[:, None] == M_BETA + head_of[None, :]).astype(BF16)
    ea = (lanes[:, None] == M_A + head_of[None, :]).astype(BF16)

    w_in_all, widths = _in_proj_layout(w_in)
    wqa, wqb, wk, wv = _mla_layout(mla_w_uq, mla_w_ukv)
    wgk = jnp.concatenate([gla_w_gk2, jnp.zeros((depth, LANE - GLA_GATE_RANK, N_HEADS * DK_B), F32)],
                          axis=1).astype(BF16)
    a_tail = LANE - M_A - N_HEADS
    apad = _cat([M_A, jnp.exp(gdn_a_log.astype(F32)), a_tail], gdn_a_log)
    dtpad = _cat([M_A, gdn_dt_bias, a_tail], gdn_dt_bias)
    w_out16, wg16, wu16, wd16 = (a.astype(BF16) for a in (w_out, ffn_w_gate, ffn_w_up, ffn_w_down))

    for l in range(depth):
        za, zb, zc, zd, zm, zm2 = _in_proj(x.reshape(t, d), pre_mix_g[l][None, :], w_in_all[l], widths, tm)
        r3 = lambda a: a.reshape(bsz, s, a.shape[-1])
        za, zb, zc, zd, zm, zm2 = map(r3, (za, zb, zc, zd, zm, zm2))

        qt, kh, vt = _mla_pre(zc, zm, zm2, mla_q_norm_g[l][None, :], mla_kv_norm_g[l][None, :],
                              wqa[l], wqb[l], wk[l], wv[l], cosq, sinq, cosk, ts)
        o_c = _mla_attn(qt, kh, vt, tq)
        o_a, o_b, o_d = _mixers(
            za, zb, zd, zm,
            (lower_bounds[l][None, :], _tile_heads(hgrn_norm_g[l])),
            (wgk[l], gla_b_gk[l][None, :], _tile_heads(gla_norm_g[l])),
            (gdn_conv_w[l], apad[l][None, :], dtpad[l][None, :], ebeta, ea, _tile_heads(gdn_norm_g[l])), tt)

        outs = [o.reshape(t, GROUP_WIDTH) for o in (o_a, o_b, o_c, o_d)]
        x = _out_proj(outs, w_out16[l], post_mix_g[l][None, :], x.reshape(t, d), tm).reshape(bsz, s, d)
        x = _ffn(x, pre_ffn_g[l][None, :], wg16[l], wu16[l], ffn_conv_w[l], wd16[l], post_ffn_g[l][None, :],
                 tm=ts, fb=fb)
    return x
```

```python
import functools

import jax
import jax.numpy as jnp
from jax import lax
from jax.experimental import pallas as pl
from jax.experimental.pallas import tpu as pltpu

F32 = jnp.float32
BF16 = jnp.bfloat16
HIGHEST = lax.Precision.HIGHEST

N_GROUPS = 4
HEAD_DIM = 64
N_HEADS = 4
GROUP_WIDTH = N_HEADS * HEAD_DIM
EPS = 1e-6
MASK_VALUE = -1e30
MIN_POS = 1e-30
LOG2E = 1.4426950408889634

DK_B = HEAD_DIM // 2
GLA_GATE_RANK = 16
GLA_GATE_NORMALIZER = 16.0
MLA_Q_RANK = 256
MLA_KV_RANK = 128
MLA_NOPE = 64
MLA_ROPE = 32
ROPE_THETA = 10000.0
GDN_CONV = 4
FFN_CONV = 3
LANE = 128
SUBLANE = 8

M_CODE = 0
M_BETA = 16
M_A = 20
M_KR = 64

BAND = 16
GDN_CHUNK = 64
VMEM_LIMIT = 56 * 1024 * 1024


def _cparams(sem):
    return pltpu.CompilerParams(dimension_semantics=sem, vmem_limit_bytes=VMEM_LIMIT)


def _rms(x, g):
    return x * lax.rsqrt(jnp.mean(x * x, axis=-1, keepdims=True) + EPS) * g


def _sigmoid(x):
    return jax.nn.sigmoid(x)


def _softplus(x):
    return jnp.maximum(x, 0.0) + jnp.log1p(jnp.exp(-jnp.abs(x)))


def _block_cumsum(x, rowmod, block):
    s = 1
    while s < block:
        x = x + jnp.where(rowmod >= s, pltpu.roll(x, s, 0), 0.0)
        s *= 2
    return x


def _dot(a, b, dims=(((1,), (0,)), ((), ())), precision=None):
    return lax.dot_general(a, b, dims, precision=precision, preferred_element_type=F32)


NT = (((1,), (1,)), ((), ()))
TN = (((0,), (0,)), ((), ()))


def _in_proj_kernel(x_ref, g_ref, w_ref, *out_refs):
    h = _rms(x_ref[...], g_ref[...]).astype(BF16)
    z = _dot(h, w_ref[...])
    off = 0
    for o_ref in out_refs:
        n = o_ref.shape[-1]
        o_ref[...] = z[:, off:off + n]
        off += n


def _in_proj(x2, g, w_all, layer, widths, tm):
    t, d = x2.shape
    n = w_all.shape[2]
    return pl.pallas_call(
        _in_proj_kernel,
        grid=(t // tm,),
        in_specs=[pl.BlockSpec((tm, d), lambda i: (i, 0)),
                  pl.BlockSpec((1, d), lambda i: (0, 0)),
                  pl.BlockSpec((None, d, n), lambda i: (layer, 0, 0))],
        out_specs=[pl.BlockSpec((tm, wd), lambda i: (i, 0)) for wd in widths],
        out_shape=[jax.ShapeDtypeStruct((t, wd), F32) for wd in widths],
        compiler_params=_cparams(("parallel",)),
        name="in_proj",
    )(x2, g, w_all)


def _gla_core(q, k, v, lg, s_ref, oacc_ref, kv_ref, sc_ref, dkh):
    tt, dk = q.shape
    dv = v.shape[1]
    half = BAND // 2
    nblk = tt // BAND
    rowmod = lax.broadcasted_iota(jnp.int32, (tt, dk), 0) & (BAND - 1)
    b = _block_cumsum(lg * LOG2E, rowmod, BAND)
    upper = rowmod >= half
    mid = jnp.broadcast_to(b.reshape(nblk, BAND, dk)[:, half - 1:half, :], (nblk, BAND, dk)).reshape(tt, dk)
    bh = jnp.where(upper, b - mid, b)
    rowmod_h = rowmod & (half - 1)
    yield

    def roll_half(x, d):
        return pltpu.roll(x.reshape(tt // half, half, x.shape[1]), d, 1).reshape(tt, x.shape[1])

    seg = (lax.broadcasted_iota(jnp.int32, (dk, dv), 0) // dkh
           == lax.broadcasted_iota(jnp.int32, (dk, dv), 1) // HEAD_DIM).astype(BF16)
    def block_kv(n):
        r0 = n * BAND
        bn = b[r0:r0 + BAND]
        kt = (k[r0:r0 + BAND] * jnp.exp2(bn[BAND - 1:BAND] - bn)).astype(BF16)
        kv_ref[n] = _dot(v[r0:r0 + BAND].astype(BF16), kt, TN)

    qe = (q * jnp.exp2(b)).astype(BF16)
    head_rows = [slice(h * HEAD_DIM, (h + 1) * HEAD_DIM) for h in range(N_HEADS)]
    head_lanes = [slice((h * dkh // LANE) * LANE, (h * dkh // LANE) * LANE + LANE) for h in range(N_HEADS)]
    own = [(lax.broadcasted_iota(jnp.int32, (HEAD_DIM, LANE), 1) + lt.start) // dkh == h
           for h, lt in enumerate(head_lanes)]
    sts = [s_ref[hs, lt] for hs, lt in zip(head_rows, head_lanes)]

    def advance(n):
        r0 = n * BAND
        for h, (hs, lt) in enumerate(zip(head_rows, head_lanes)):
            sc_ref[n, hs, lt] = jnp.where(own[h], sts[h], 0.0).astype(BF16)
            sts[h] = sts[h] * jnp.exp2(b[r0 + BAND - 1:r0 + BAND, lt]) + kv_ref[n, hs, lt]
        oacc_ref[r0:r0 + BAND, :] = _dot(qe[r0:r0 + BAND], sc_ref[n], NT)

    q_up = (q * jnp.exp2(jnp.where(upper, bh, MASK_VALUE))).astype(BF16)
    k_lo = (k * jnp.exp2(jnp.where(upper, MASK_VALUE, mid - b))).astype(BF16)
    v16 = v.astype(BF16)
    same_blk = (lax.broadcasted_iota(jnp.int32, (tt, tt), 0) // BAND
                == lax.broadcasted_iota(jnp.int32, (tt, tt), 1) // BAND)
    cross = [jnp.where(same_blk, _dot(q_up[:, h * dkh:(h + 1) * dkh], k_lo[:, h * dkh:(h + 1) * dkh], NT), 0.0)
             for h in range(N_HEADS)]
    cross_out = [_dot(a.astype(BF16), v16[:, h * HEAD_DIM:(h + 1) * HEAD_DIM]) for h, a in enumerate(cross)]
    yield

    per_step = -(-nblk // half)
    o = _dot((q * k).astype(BF16), seg) * v
    for n in range(per_step):
        block_kv(n)
    for d in range(1, half):
        kd = roll_half(k, d)
        bd = roll_half(bh, d)
        vd = roll_half(v, d)
        p = q * kd * jnp.exp2(jnp.where(rowmod_h >= d, bh - bd, MASK_VALUE))
        o = o + _dot(p.astype(BF16), seg) * vd
        for n in range(d * per_step, min((d + 1) * per_step, nblk)):
            block_kv(n)
        for n in range((d - 1) * per_step, min(d * per_step, nblk)):
            advance(n)
        yield
    for n in range((half - 1) * per_step, nblk):
        advance(n)
    for h, (hs, lt) in enumerate(zip(head_rows, head_lanes)):
        s_ref[hs, lt] = sts[h]

    for h, c_out in enumerate(cross_out):
        hv = slice(h * HEAD_DIM, (h + 1) * HEAD_DIM)
        oacc_ref[:, hv] = oacc_ref[:, hv] + c_out
    return o + oacc_ref[...]


def _head_sum(x):
    gw = GROUP_WIDTH
    seg = (lax.broadcasted_iota(jnp.int32, (gw, gw), 0) // HEAD_DIM
           == lax.broadcasted_iota(jnp.int32, (gw, gw), 1) // HEAD_DIM).astype(BF16)
    hi = x.astype(BF16)
    lo = (x - hi.astype(F32)).astype(BF16)
    return _dot(hi, seg) + _dot(lo, seg)


def _head_norm_gate(o, ng, gate_act, o_ref):
    ms = _head_sum(o * o) * (1.0 / HEAD_DIM)
    o_ref[0] = (o * lax.rsqrt(ms + EPS) * ng * gate_act).astype(o_ref.dtype)


def _hgrn_body(z_ref, lb_ref, ng_ref, o_ref, s_ref, oacc_ref, kv_ref, sc_ref):
    gw = GROUP_WIDTH
    q = z_ref[0, :, 0:gw]
    zf = z_ref[0, :, gw:2 * gw]
    v = z_ref[0, :, 2 * gw:3 * gw]
    gate = z_ref[0, :, 3 * gw:4 * gw]
    lb = lb_ref[...]
    f = lb + (1.0 - lb) * _sigmoid(zf)
    lg = jnp.log(jnp.maximum(f, MIN_POS))
    k = (1.0 - lb) * _sigmoid(-zf)
    qh = q * _sigmoid(q) * HEAD_DIM ** -0.5
    yield
    o = yield from _gla_core(qh, k, v, lg, s_ref, oacc_ref, kv_ref, sc_ref, HEAD_DIM)
    yield
    _head_norm_gate(o, ng_ref[...], _sigmoid(gate), o_ref)


def _gla_body(z_ref, m_ref, wgk_ref, bgk_ref, ng_ref, o_ref, s_ref, oacc_ref, kv_ref, sc_ref):
    dk = N_HEADS * DK_B
    q = z_ref[0, :, 0:dk]
    k = z_ref[0, :, dk:2 * dk]
    v = z_ref[0, :, 2 * dk:2 * dk + GROUP_WIDTH]
    gate = z_ref[0, :, 2 * dk + GROUP_WIDTH:2 * dk + 2 * GROUP_WIDTH]
    x = _dot(m_ref[0].astype(BF16), wgk_ref[...]) + bgk_ref[...]
    lg = -_softplus(-x) / GLA_GATE_NORMALIZER
    yield
    o = yield from _gla_core(q * DK_B ** -0.5, k, v, lg, s_ref, oacc_ref, kv_ref, sc_ref, DK_B)
    yield
    _head_norm_gate(o, ng_ref[...], gate * _sigmoid(gate), o_ref)


def _row(a):
    return pl.BlockSpec(a.shape, lambda b, t: (0,) * a.ndim)


def _gla_scratch(tt, dk):
    nblk = tt // BAND
    return [pltpu.VMEM((GROUP_WIDTH, dk), F32),
            pltpu.VMEM((tt, GROUP_WIDTH), F32),
            pltpu.VMEM((nblk, GROUP_WIDTH, dk), F32),
            pltpu.VMEM((nblk, GROUP_WIDTH, dk), BF16)]


def _gdn_body(z_ref, m_ref, cw_ref, apad_ref, dtpad_ref, ebeta_ref, ea_ref, ng_ref,
              o_ref, xs_ref, s_ref, oacc_ref, u_ref, w_ref, aqk_ref, sc_ref):
    tt = z_ref.shape[1]
    gw = GROUP_WIDTH
    c = GDN_CHUNK
    halo = SUBLANE

    xs_ref[halo:halo + tt, :] = z_ref[0, :, 0:3 * gw]
    acc = cw_ref[GDN_CONV - 1:GDN_CONV, :] * xs_ref[halo:halo + tt, :]
    for j in range(1, GDN_CONV):
        acc = acc + cw_ref[GDN_CONV - 1 - j:GDN_CONV - j, :] * xs_ref[halo - j:halo - j + tt, :]
    xs_ref[0:halo, :] = xs_ref[tt:tt + halo, :]
    qkv = acc * _sigmoid(acc)
    yield

    q_all, k_all, v_all = qkv[:, 0:gw], qkv[:, gw:2 * gw], qkv[:, 2 * gw:3 * gw]

    head_blk = (lax.broadcasted_iota(jnp.int32, (gw, gw), 0) // HEAD_DIM
                == lax.broadcasted_iota(jnp.int32, (gw, gw), 1) // HEAD_DIM)
    def expand(x, e_ref, terms):
        acc, r = None, x
        for _ in range(terms):
            hi = r.astype(BF16)
            r = r - hi.astype(F32)
            part = _dot(hi, e_ref[...])
            acc = part if acc is None else acc + part
        return acc

    qn = q_all * lax.rsqrt(_head_sum(q_all * q_all) + EPS) * HEAD_DIM ** -0.5
    kn = k_all * lax.rsqrt(_head_sum(k_all * k_all) + EPS)
    yield

    m = m_ref[0]
    beta_x = expand(_sigmoid(m), ebeta_ref, 2)
    lg_m = -apad_ref[...] * _softplus(m + dtpad_ref[...])
    rowmod_m = lax.broadcasted_iota(jnp.int32, (tt, LANE), 0) & (c - 1)
    b_m = _block_cumsum(lg_m, rowmod_m, c)
    b_x = expand(b_m, ea_ref, 3)
    eb = jnp.exp(b_x)
    kb = kn * beta_x
    vb = v_all * beta_x
    kbe = kb * eb
    qe = qn * eb
    yield

    ii = lax.broadcasted_iota(jnp.int32, (tt, tt), 0)
    jj = lax.broadcasted_iota(jnp.int32, (tt, tt), 1)
    same = (ii // c) == (jj // c)
    incl = same & (ii >= jj)
    strict = same & (ii > jj)

    heads = [slice(h * HEAD_DIM, (h + 1) * HEAD_DIM) for h in range(N_HEADS)]
    nms = []
    for h, sl in enumerate(heads):
        bmat = jnp.broadcast_to(b_m[:, M_A + h:M_A + h + 1], (tt, tt))
        lmask = jnp.where(incl, jnp.exp(jnp.minimum(bmat - bmat.T, 0.0)), 0.0)
        kh16 = kn[:, sl].astype(BF16)
        nms.append(jnp.where(strict, -_dot(kb[:, sl].astype(BF16), kh16, NT) * lmask, 0.0))
        aqk_ref[h] = (_dot(qn[:, sl].astype(BF16), kh16, NT) * lmask).astype(BF16)
        if h % 2 == 1:
            yield
    tns = list(nms)
    pws = list(nms)
    for _ in range(5):
        pws16 = [p.astype(BF16) for p in pws]
        pws = [_dot(p, p) for p in pws16]
        tns = [t + p + _dot(t.astype(BF16), p.astype(BF16)) for t, p in zip(tns, pws)]
        yield
    for sl, tn in zip(heads, tns):
        tn16 = tn.astype(BF16)
        u_ref[:, sl] = vb[:, sl] + _dot(tn16, vb[:, sl].astype(BF16))
        w_ref[:, sl] = kbe[:, sl] + _dot(tn16, kbe[:, sl].astype(BF16))
    yield

    decays, pms, rms = [], [], []
    for n in range(tt // c):
        rs = slice(n * c, (n + 1) * c)
        b_last = b_x[n * c + c - 1:(n + 1) * c, :]
        kt_t = (kn[rs] * jnp.exp(b_last - b_x[rs])).T.astype(BF16)
        decays.append(jnp.exp(b_last))
        pms.append(jnp.where(head_blk, _dot(kt_t, w_ref[rs, :].astype(BF16)), 0.0).astype(BF16))
        rms.append(jnp.where(head_blk, _dot(kt_t, u_ref[rs, :].astype(BF16)), 0.0))
    yield
    st = s_ref[...]
    for n in range(tt // c):
        st16 = st.astype(BF16)
        sc_ref[n] = st16
        st = st * decays[n] - _dot(pms[n], st16) + rms[n]
    s_ref[...] = st
    yield

    for n in range(tt // c):
        rs = slice(n * c, (n + 1) * c)
        lhs = jnp.concatenate([w_ref[rs, :], qe[rs]], axis=0).astype(BF16)
        res = _dot(lhs, sc_ref[n])
        u_ref[rs, :] = u_ref[rs, :] - res[0:c]
        oacc_ref[rs, :] = res[c:2 * c]
    for h in range(N_HEADS):
        sl = slice(h * HEAD_DIM, (h + 1) * HEAD_DIM)
        oacc_ref[:, sl] = oacc_ref[:, sl] + _dot(aqk_ref[h], u_ref[:, sl].astype(BF16))
    yield

    zg = z_ref[0, :, 3 * gw:4 * gw]
    _head_norm_gate(oacc_ref[...], ng_ref[...], zg * _sigmoid(zg), o_ref)


def _interleave(*stages):
    live = list(stages)
    while live:
        for g in list(live):
            try:
                next(g)
            except StopIteration:
                live.remove(g)


N_HGRN_IN, N_GLA_IN, N_GDN_IN = 3, 5, 8
N_GLA_SCRATCH, N_GDN_SCRATCH = 4, 7
GDN_LEAD_STAGES = 6


def _mixers_kernel(*refs):
    it = iter(refs)
    take = lambda n: [next(it) for _ in range(n)]
    hgrn_in, gla_in, gdn_in = take(N_HGRN_IN), take(N_GLA_IN), take(N_GDN_IN)
    oa_ref, ob_ref, od_ref = take(3)
    hgrn_scr, gla_scr, gdn_scr = take(N_GLA_SCRATCH), take(N_GLA_SCRATCH), take(N_GDN_SCRATCH)

    @pl.when(pl.program_id(1) == 0)
    def _():
        for scr in (hgrn_scr, gla_scr):
            scr[0][...] = jnp.zeros_like(scr[0])
            scr[3][...] = jnp.zeros_like(scr[3])
        gdn_scr[1][...] = jnp.zeros_like(gdn_scr[1])
        gdn_scr[0][0:SUBLANE, :] = jnp.zeros((SUBLANE, gdn_scr[0].shape[1]), F32)

    gdn = _gdn_body(*gdn_in, od_ref, *gdn_scr)
    for _ in range(GDN_LEAD_STAGES):
        next(gdn)
    _interleave(gdn,
                _hgrn_body(*hgrn_in, oa_ref, *hgrn_scr),
                _gla_body(*gla_in, ob_ref, *gla_scr))


def _mixers(za, zb, zd, zm, hgrn_params, gla_params, gdn_params, tt):
    bsz, s, _ = za.shape
    gw = GROUP_WIDTH
    tok = lambda a: pl.BlockSpec((1, tt, a.shape[2]), lambda b, t: (b, t, 0))
    operands = [za, *hgrn_params, zb, zm, *gla_params, zd, zm, *gdn_params]
    in_specs = ([tok(za)] + [_row(a) for a in hgrn_params]
                + [tok(zb), tok(zm)] + [_row(a) for a in gla_params]
                + [tok(zd), tok(zm)] + [_row(a) for a in gdn_params])
    assert (len(hgrn_params) + 1, len(gla_params) + 2, len(gdn_params) + 2) == (N_HGRN_IN, N_GLA_IN, N_GDN_IN)
    out = pl.BlockSpec((1, tt, gw), lambda b, t: (b, t, 0))
    gdn_scratch = [pltpu.VMEM((tt + 2 * SUBLANE, 3 * gw), F32),
                   pltpu.VMEM((gw, gw), F32),
                   pltpu.VMEM((tt, gw), F32),
                   pltpu.VMEM((tt, gw), F32),
                   pltpu.VMEM((tt, gw), F32),
                   pltpu.VMEM((N_HEADS, tt, tt), BF16),
                   pltpu.VMEM((tt // GDN_CHUNK, gw, gw), BF16)]
    assert len(gdn_scratch) == N_GDN_SCRATCH and len(_gla_scratch(tt, gw)) == N_GLA_SCRATCH
    return pl.pallas_call(
        _mixers_kernel,
        grid=(bsz, s // tt),
        in_specs=in_specs,
        out_specs=[out, out, out],
        out_shape=[jax.ShapeDtypeStruct((bsz, s, gw), BF16)] * 3,
        scratch_shapes=_gla_scratch(tt, gw) + _gla_scratch(tt, N_HEADS * DK_B) + gdn_scratch,
        compiler_params=_cparams(("parallel", "arbitrary")),
        name="mixers",
    )(*operands)


def _mla_pre_kernel(zc_ref, m1_ref, m2_ref, gq_ref, gkv_ref, wqa_ref, wqb_ref, wk_ref, wv_ref,
                    cosq_ref, sinq_ref, cosk_ref, qt_ref, k_ref, vt_ref):
    scale = (MLA_NOPE + MLA_ROPE) ** -0.5 * LOG2E
    cq = _rms(zc_ref[0, :, 0:MLA_Q_RANK], gq_ref[...]).astype(BF16)
    ckv = _rms(zc_ref[0, :, MLA_Q_RANK:MLA_Q_RANK + MLA_KV_RANK], gkv_ref[...]).astype(BF16)
    qa = _dot(cq, wqa_ref[...])
    qb = _dot(cq, wqb_ref[...])
    kn = _dot(ckv, wk_ref[...])
    vv = _dot(ckv, wv_ref[...])
    cosq, sinq = cosq_ref[...], sinq_ref[...]
    kr = m1_ref[0] * cosk_ref[...] + m2_ref[0] * sinq
    for h in range(N_HEADS):
        sl = slice(h * LANE, (h + 1) * LANE)
        q = (qa[:, sl] * cosq + qb[:, sl] * sinq) * scale
        qt_ref[0, sl, :] = q.T.astype(BF16)
        k_ref[0, h] = (kn[:, sl] + kr).astype(BF16)
    vt_ref[0] = vv.T.astype(BF16)


def _mla_pre(zc, zm, zm2, gq, gkv, wqa, wqb, wk, wv, cosq, sinq, cosk, tm):
    bsz, s, _ = zc.shape
    tok = lambda wd: pl.BlockSpec((1, tm, wd), lambda b, t: (b, t, 0))
    tab = pl.BlockSpec((tm, LANE), lambda b, t: (t, 0))
    tr = lambda rows: pl.BlockSpec((1, rows, tm), lambda b, t: (b, 0, t))
    return pl.pallas_call(
        _mla_pre_kernel,
        grid=(bsz, s // tm),
        in_specs=[tok(zc.shape[2]), tok(LANE), tok(LANE), _row(gq), _row(gkv),
                  _row(wqa), _row(wqb), _row(wk), _row(wv), tab, tab, tab],
        out_specs=[tr(N_HEADS * LANE),
                   pl.BlockSpec((1, N_HEADS, tm, LANE), lambda b, t: (b, 0, t, 0)),
                   tr(GROUP_WIDTH)],
        out_shape=[jax.ShapeDtypeStruct((bsz, N_HEADS * LANE, s), BF16),
                   jax.ShapeDtypeStruct((bsz, N_HEADS, s, LANE), BF16),
                   jax.ShapeDtypeStruct((bsz, GROUP_WIDTH, s), BF16)],
        compiler_params=_cparams(("parallel", "parallel")),
        name="mla_pre",
    )(zc, zm, zm2, gq, gkv, wqa, wqb, wk, wv, cosq, sinq, cosk)


def _mla_attn_kernel(qi_ref, kj_ref, qt_ref, k_ref, vt_ref, o_ref, m_ref, l_ref, acc_ref):
    pair = pl.program_id(1)
    i = qi_ref[pair]
    j = kj_ref[pair]
    tq = qt_ref.shape[2]
    tk = k_ref.shape[2]

    @pl.when(j == 0)
    def _():
        m_ref[...] = jnp.full_like(m_ref, MASK_VALUE)
        l_ref[...] = jnp.zeros_like(l_ref)
        acc_ref[...] = jnp.zeros_like(acc_ref)

    def sub_step(k0, q0, masked):
        nk = tk // 2
        ks = slice(k0, k0 + nk)
        qs = slice(q0, tq)
        scores = [_dot(k_ref[0, h, ks, :], qt_ref[0, h * LANE:(h + 1) * LANE, qs]) for h in range(N_HEADS)]
        if masked:
            keep = (lax.broadcasted_iota(jnp.int32, (nk, tq - q0), 0) + k0
                    <= lax.broadcasted_iota(jnp.int32, (nk, tq - q0), 1) + q0)
            scores = [jnp.where(keep, s, MASK_VALUE) for s in scores]
        m_prevs = [m_ref[h:h + 1, qs] for h in range(N_HEADS)]
        m_news = [jnp.maximum(mp, jnp.max(s, axis=0, keepdims=True)) for mp, s in zip(m_prevs, scores)]
        ps = [jnp.exp2(s - mn) for s, mn in zip(scores, m_news)]
        for h in range(N_HEADS):
            alpha = jnp.exp2(m_prevs[h] - m_news[h])
            l_ref[h:h + 1, qs] = alpha * l_ref[h:h + 1, qs] + jnp.sum(ps[h], axis=0, keepdims=True)
            hs = slice(h * HEAD_DIM, (h + 1) * HEAD_DIM)
            acc_ref[hs, qs] = alpha * acc_ref[hs, qs] + _dot(vt_ref[0, hs, ks], ps[h].astype(BF16))
            m_ref[h:h + 1, qs] = m_news[h]

    @pl.when(j < i)
    def _():
        sub_step(0, 0, False)
        sub_step(tk // 2, 0, False)

    @pl.when(j == i)
    def _():
        sub_step(0, 0, True)
        sub_step(tk // 2, tq // 2, True)
        for h in range(N_HEADS):
            hs = slice(h * HEAD_DIM, (h + 1) * HEAD_DIM)
            acc_ref[hs, :] = acc_ref[hs, :] / l_ref[h:h + 1, :]
        o_ref[0] = acc_ref[...].T.astype(o_ref.dtype)


def _mla_attn(qt, k, vt, tq):
    bsz, _, s = qt.shape
    nq = s // tq
    pairs = [(i, j) for i in range(nq) for j in range(i + 1)]
    qi = jnp.array([p[0] for p in pairs], jnp.int32)
    kj = jnp.array([p[1] for p in pairs], jnp.int32)
    grid_spec = pltpu.PrefetchScalarGridSpec(
        num_scalar_prefetch=2,
        grid=(bsz, len(pairs)),
        in_specs=[pl.BlockSpec((1, N_HEADS * LANE, tq), lambda b, p, qi, kj: (b, 0, qi[p])),
                  pl.BlockSpec((1, N_HEADS, tq, LANE), lambda b, p, qi, kj: (b, 0, kj[p], 0)),
                  pl.BlockSpec((1, GROUP_WIDTH, tq), lambda b, p, qi, kj: (b, 0, kj[p]))],
        out_specs=pl.BlockSpec((1, tq, GROUP_WIDTH), lambda b, p, qi, kj: (b, qi[p], 0)),
        scratch_shapes=[pltpu.VMEM((SUBLANE, tq), F32), pltpu.VMEM((SUBLANE, tq), F32),
                        pltpu.VMEM((GROUP_WIDTH, tq), F32)],
    )
    return pl.pallas_call(
        _mla_attn_kernel,
        grid_spec=grid_spec,
        out_shape=jax.ShapeDtypeStruct((bsz, s, GROUP_WIDTH), BF16),
        compiler_params=_cparams(("parallel", "arbitrary")),
        name="mla_attn",
    )(qi, kj, qt, k, vt)


def _gelu_tanh(x):
    return 0.5 * x * (1.0 + jnp.tanh(0.7978845608028654 * (x + 0.044715 * (x * x * x))))


def _ffn_kernel(oa_ref, ob_ref, oc_ref, od_ref, wo_ref, g0_ref, x_ref,
                g1_ref, wg_ref, wu_ref, cw_ref, wd_ref, g2_ref, y_ref, prev_ref, act_ref, *, fb):
    tm = x_ref.shape[1]
    f = wg_ref.shape[1]

    @pl.when(pl.program_id(1) == 0)
    def _():
        prev_ref[...] = jnp.zeros_like(prev_ref)

    mix = jnp.concatenate([oa_ref[0], ob_ref[0], oc_ref[0], od_ref[0]], axis=1)
    x = x_ref[0] + _rms(_dot(mix, wo_ref[...]), g0_ref[...])
    h = _rms(x, g1_ref[...]).astype(BF16)
    row = lax.broadcasted_iota(jnp.int32, (tm, fb), 0)
    for n in range(f // fb):
        cs = slice(n * fb, (n + 1) * fb)
        gate = _dot(h, wg_ref[:, cs])
        up = _dot(h, wu_ref[:, cs])
        p1 = prev_ref[1:2, cs]
        p2 = prev_ref[0:1, cs]
        g1 = jnp.where(row == 0, p1, pltpu.roll(gate, 1, 0))
        g2 = jnp.where(row == 0, p2, jnp.where(row == 1, p1, pltpu.roll(gate, 2, 0)))
        prev_ref[0:2, cs] = gate[tm - 2:tm, :]
        conv = cw_ref[2:3, cs] * gate + cw_ref[1:2, cs] * g1 + cw_ref[0:1, cs] * g2
        act_ref[:, cs] = (_gelu_tanh(conv) * up).astype(BF16)
    y_ref[0] = x + _rms(_dot(act_ref[...], wd_ref[...]), g2_ref[...])


def _ffn(outs, wo, g0, x, g1, wg, wu, cw, wd, g2, layer, tm, fb):
    bsz, s, d = x.shape
    grp = pl.BlockSpec((1, tm, GROUP_WIDTH), lambda b, t: (b, t, 0))
    once = lambda a: pl.BlockSpec((None,) + a.shape[1:], lambda b, t: (layer, 0, 0), pipeline_mode=pl.Buffered(1))
    return pl.pallas_call(
        functools.partial(_ffn_kernel, fb=fb),
        grid=(bsz, s // tm),
        in_specs=[grp, grp, grp, grp, once(wo), _row(g0),
                  pl.BlockSpec((1, tm, d), lambda b, t: (b, t, 0)),
                  _row(g1), once(wg), once(wu), _row(cw), once(wd), _row(g2)],
        out_specs=pl.BlockSpec((1, tm, d), lambda b, t: (b, t, 0)),
        out_shape=jax.ShapeDtypeStruct((bsz, s, d), F32),
        scratch_shapes=[pltpu.VMEM((SUBLANE, wg.shape[2]), F32),
                        pltpu.VMEM((tm, wg.shape[2]), BF16)],
        compiler_params=_cparams(("parallel", "arbitrary")),
        name="ffn",
    )(*outs, wo, g0, x, g1, wg, wu, cw, wd, g2)


def _cat(pieces, like):
    lead = like.shape[:-1]
    return jnp.concatenate([jnp.zeros(lead + (p,), like.dtype) if isinstance(p, int) else p for p in pieces], axis=-1)


def _rot_cols(w):
    half = w.shape[-1] // 2
    return jnp.concatenate([-w[..., half:], w[..., :half]], axis=-1)


def _in_proj_layout(w):
    gw, dkb = GROUP_WIDTH, N_HEADS * DK_B
    o_b = 4 * gw
    o_code = o_b + 2 * dkb + gw
    o_bg = o_code + GLA_GATE_RANK
    o_c = o_bg + gw
    o_kr = o_c + MLA_Q_RANK + MLA_KV_RANK
    o_d = o_kr + MLA_ROPE
    o_beta = o_d + 3 * gw
    o_a = o_beta + N_HEADS
    o_z = o_a + N_HEADS
    assert o_z + gw == w.shape[-1]
    assert (M_CODE, M_BETA, M_A) == (0, GLA_GATE_RANK, GLA_GATE_RANK + N_HEADS)
    kr = w[..., o_kr:o_d]
    misc_gap = M_KR - (M_A + N_HEADS)
    tail_gap = LANE - M_KR - MLA_ROPE
    cols = [w[..., 0:o_b],
            w[..., o_b:o_code], w[..., o_bg:o_c],
            w[..., o_c:o_kr],
            w[..., o_d:o_beta], w[..., o_z:],
            w[..., o_code:o_bg], w[..., o_beta:o_a], w[..., o_a:o_z], misc_gap, kr, tail_gap,
            M_KR, _rot_cols(kr), tail_gap]
    widths = (4 * gw, 2 * dkb + 2 * gw, MLA_Q_RANK + MLA_KV_RANK, 4 * gw, LANE, LANE)
    return _cat(cols, w).astype(BF16), widths


def _mla_layout(w_uq, w_ukv):
    dq = MLA_NOPE + MLA_ROPE
    pad = LANE - dq
    qa, qb, wk, wv = [], [], [], []
    for h in range(N_HEADS):
        nope = w_uq[..., h * dq:h * dq + MLA_NOPE]
        rope = w_uq[..., h * dq + MLA_NOPE:(h + 1) * dq]
        qa += [nope, rope, pad]
        qb += [MLA_NOPE, _rot_cols(rope), pad]
        wk += [w_ukv[..., h * 2 * HEAD_DIM:h * 2 * HEAD_DIM + MLA_NOPE], LANE - MLA_NOPE]
        wv += [w_ukv[..., h * 2 * HEAD_DIM + MLA_NOPE:(h + 1) * 2 * HEAD_DIM]]
    return tuple(_cat(p, like).astype(BF16) for p, like in ((qa, w_uq), (qb, w_uq), (wk, w_ukv), (wv, w_ukv)))


def _rope_tables(s):
    inv = ROPE_THETA ** (-jnp.arange(0, MLA_ROPE, 2, dtype=F32) / MLA_ROPE)
    ang = jnp.arange(s, dtype=F32)[:, None] * inv[None, :]
    cos, sin = jnp.cos(ang), jnp.sin(ang)
    pad = LANE - MLA_NOPE - MLA_ROPE
    cosq = _cat([jnp.ones((s, MLA_NOPE), F32), cos, cos, pad], cos)
    sinq = _cat([MLA_NOPE, sin, sin, pad], cos)
    cosk = _cat([MLA_NOPE, cos, cos, pad], cos)
    return cosq, sinq, cosk


def _tile_heads(g):
    return jnp.tile(g, N_HEADS)[None, :]


def _pick(n, prefs):
    for p in prefs:
        if n % p == 0:
            return p
    return n


def kernel(x, w_in, w_out, pre_mix_g, post_mix_g, pre_ffn_g, post_ffn_g, hgrn_lb_logits, hgrn_norm_g,
           gla_w_gk2, gla_b_gk, gla_norm_g, mla_q_norm_g, mla_w_uq, mla_kv_norm_g, mla_w_ukv,
           gdn_conv_w, gdn_a_log, gdn_dt_bias, gdn_norm_g, ffn_w_gate, ffn_w_up, ffn_conv_w, ffn_w_down):
    bsz, s, d = x.shape
    depth = w_in.shape[0]
    t = bsz * s
    tm = _pick(t, (512, 256))
    tt = _pick(s, (256,))
    ts = _pick(s, (512, 256))
    tq = _pick(s, (1024, 512, 256, 128))
    fb = 256

    cosq, sinq, cosk = _rope_tables(s)
    p = jax.nn.softmax(hgrn_lb_logits.astype(F32), axis=0)
    lower_bounds = jnp.cumsum(p, axis=0) - p[0]

    lanes = jnp.arange(LANE)
    head_of = jnp.arange(GROUP_WIDTH) // HEAD_DIM
    ebeta = (lanes[:, None] == M_BETA + head_of[None, :]).astype(BF16)
    ea = (lanes[:, None] == M_A + head_of[None, :]).astype(BF16)

    w_in_all, widths = _in_proj_layout(w_in)
    wqa, wqb, wk, wv = _mla_layout(mla_w_uq, mla_w_ukv)
    wgk = jnp.concatenate([gla_w_gk2, jnp.zeros((depth, LANE - GLA_GATE_RANK, N_HEADS * DK_B), F32)],
                          axis=1).astype(BF16)
    a_tail = LANE - M_A - N_HEADS
    apad = _cat([M_A, jnp.exp(gdn_a_log.astype(F32)), a_tail], gdn_a_log)
    dtpad = _cat([M_A, gdn_dt_bias, a_tail], gdn_dt_bias)
    w_out16, wg16, wu16, wd16 = (a.astype(BF16) for a in (w_out, ffn_w_gate, ffn_w_up, ffn_w_down))

    for l in range(depth):
        za, zb, zc, zd, zm, zm2 = _in_proj(x.reshape(t, d), pre_mix_g[l][None, :], w_in_all, l, widths, tm)
        r3 = lambda a: a.reshape(bsz, s, a.shape[-1])
        za, zb, zc, zd, zm, zm2 = map(r3, (za, zb, zc, zd, zm, zm2))

        qt, kh, vt = _mla_pre(zc, zm, zm2, mla_q_norm_g[l][None, :], mla_kv_norm_g[l][None, :],
                              wqa[l], wqb[l], wk[l], wv[l], cosq, sinq, cosk, ts)
        o_c = _mla_attn(qt, kh, vt, tq)
        o_a, o_b, o_d = _mixers(
            za, zb, zd, zm,
            (lower_bounds[l][None, :], _tile_heads(hgrn_norm_g[l])),
            (wgk[l], gla_b_gk[l][None, :], _tile_heads(gla_norm_g[l])),
            (gdn_conv_w[l], apad[l][None, :], dtpad[l][None, :], ebeta, ea, _tile_heads(gdn_norm_g[l])), tt)

        x = _ffn((o_a, o_b, o_c, o_d), w_out16, post_mix_g[l][None, :], x,
                 pre_ffn_g[l][None, :], wg16, wu16, ffn_conv_w[l], wd16, post_ffn_g[l][None, :],
                 layer=l, tm=ts, fb=fb)
    return x
```

```python
import functools

import jax
import jax.numpy as jnp
from jax import lax
from jax.experimental import pallas as pl
from jax.experimental.pallas import tpu as pltpu

F32 = jnp.float32
BF16 = jnp.bfloat16
HIGHEST = lax.Precision.HIGHEST

N_GROUPS = 4
HEAD_DIM = 64
N_HEADS = 4
GROUP_WIDTH = N_HEADS * HEAD_DIM
EPS = 1e-6
MASK_VALUE = -1e30
MIN_POS = 1e-30
LOG2E = 1.4426950408889634

DK_B = HEAD_DIM // 2
GLA_GATE_RANK = 16
GLA_GATE_NORMALIZER = 16.0
MLA_Q_RANK = 256
MLA_KV_RANK = 128
MLA_NOPE = 64
MLA_ROPE = 32
ROPE_THETA = 10000.0
GDN_CONV = 4
FFN_CONV = 3
LANE = 128
SUBLANE = 8

M_CODE = 0
M_BETA = 16
M_A = 20
M_KR = 64

BAND = 16
GDN_CHUNK = 64
V_ROWS = HEAD_DIM + 16
VMEM_LIMIT = 56 * 1024 * 1024


def _cparams(sem):
    return pltpu.CompilerParams(dimension_semantics=sem, vmem_limit_bytes=VMEM_LIMIT)


def _rms(x, g):
    return x * lax.rsqrt(jnp.mean(x * x, axis=-1, keepdims=True) + EPS) * g


def _sigmoid(x):
    return jax.nn.sigmoid(x)


def _softplus(x):
    return jnp.maximum(x, 0.0) + jnp.log1p(jnp.exp(-jnp.abs(x)))


def _block_cumsum(x, rowmod, block):
    s = 1
    while s < block:
        x = x + jnp.where(rowmod >= s, pltpu.roll(x, s, 0), 0.0)
        s *= 2
    return x


def _dot(a, b, dims=(((1,), (0,)), ((), ())), precision=None):
    return lax.dot_general(a, b, dims, precision=precision, preferred_element_type=F32)


NT = (((1,), (1,)), ((), ()))
TN = (((0,), (0,)), ((), ()))


def _in_proj_kernel(x_ref, g_ref, w_ref, *out_refs):
    h = _rms(x_ref[...], g_ref[...]).astype(BF16)
    z = _dot(h, w_ref[...])
    off = 0
    for o_ref in out_refs:
        n = o_ref.shape[-1]
        o_ref[...] = z[:, off:off + n]
        off += n


def _in_proj(x2, g, w_all, layer, widths, tm):
    t, d = x2.shape
    n = w_all.shape[2]
    return pl.pallas_call(
        _in_proj_kernel,
        grid=(t // tm,),
        in_specs=[pl.BlockSpec((tm, d), lambda i: (i, 0)),
                  pl.BlockSpec((1, d), lambda i: (0, 0)),
                  pl.BlockSpec((None, d, n), lambda i: (layer, 0, 0))],
        out_specs=[pl.BlockSpec((tm, wd), lambda i: (i, 0)) for wd in widths],
        out_shape=[jax.ShapeDtypeStruct((t, wd), F32) for wd in widths],
        compiler_params=_cparams(("parallel",)),
        name="in_proj",
    )(x2, g, w_all)


def _gla_core(q, k, v, lg, s_ref, oacc_ref, kv_ref, sc_ref, dkh):
    tt, dk = q.shape
    dv = v.shape[1]
    half = BAND // 2
    nblk = tt // BAND
    rowmod = lax.broadcasted_iota(jnp.int32, (tt, dk), 0) & (BAND - 1)
    b = _block_cumsum(lg * LOG2E, rowmod, BAND)
    upper = rowmod >= half
    mid = jnp.broadcast_to(b.reshape(nblk, BAND, dk)[:, half - 1:half, :], (nblk, BAND, dk)).reshape(tt, dk)
    bh = jnp.where(upper, b - mid, b)
    rowmod_h = rowmod & (half - 1)
    yield

    def roll_half(x, d):
        return pltpu.roll(x.reshape(tt // half, half, x.shape[1]), d, 1).reshape(tt, x.shape[1])

    seg = (lax.broadcasted_iota(jnp.int32, (dk, dv), 0) // dkh
           == lax.broadcasted_iota(jnp.int32, (dk, dv), 1) // HEAD_DIM).astype(BF16)
    def block_kv(n):
        r0 = n * BAND
        bn = b[r0:r0 + BAND]
        kt = (k[r0:r0 + BAND] * jnp.exp2(bn[BAND - 1:BAND] - bn)).astype(BF16)
        kv_ref[n] = _dot(v[r0:r0 + BAND].astype(BF16), kt, TN)

    qe = (q * jnp.exp2(b)).astype(BF16)
    head_rows = [slice(h * HEAD_DIM, (h + 1) * HEAD_DIM) for h in range(N_HEADS)]
    head_lanes = [slice((h * dkh // LANE) * LANE, (h * dkh // LANE) * LANE + LANE) for h in range(N_HEADS)]
    own = [(lax.broadcasted_iota(jnp.int32, (HEAD_DIM, LANE), 1) + lt.start) // dkh == h
           for h, lt in enumerate(head_lanes)]
    sts = [s_ref[hs, lt] for hs, lt in zip(head_rows, head_lanes)]

    def advance(n):
        r0 = n * BAND
        for h, (hs, lt) in enumerate(zip(head_rows, head_lanes)):
            sc_ref[n, hs, lt] = jnp.where(own[h], sts[h], 0.0).astype(BF16)
            sts[h] = sts[h] * jnp.exp2(b[r0 + BAND - 1:r0 + BAND, lt]) + kv_ref[n, hs, lt]
        oacc_ref[r0:r0 + BAND, :] = _dot(qe[r0:r0 + BAND], sc_ref[n], NT)

    q_up = (q * jnp.exp2(jnp.where(upper, bh, MASK_VALUE))).astype(BF16)
    k_lo = (k * jnp.exp2(jnp.where(upper, MASK_VALUE, mid - b))).astype(BF16)
    v16 = v.astype(BF16)
    same_blk = (lax.broadcasted_iota(jnp.int32, (tt, tt), 0) // BAND
                == lax.broadcasted_iota(jnp.int32, (tt, tt), 1) // BAND)
    cross = [jnp.where(same_blk, _dot(q_up[:, h * dkh:(h + 1) * dkh], k_lo[:, h * dkh:(h + 1) * dkh], NT), 0.0)
             for h in range(N_HEADS)]
    cross_out = [_dot(a.astype(BF16), v16[:, h * HEAD_DIM:(h + 1) * HEAD_DIM]) for h, a in enumerate(cross)]
    yield

    per_step = -(-nblk // half)
    o = _dot((q * k).astype(BF16), seg) * v
    for n in range(per_step):
        block_kv(n)
    for d in range(1, half):
        kd = roll_half(k, d)
        bd = roll_half(bh, d)
        vd = roll_half(v, d)
        p = q * kd * jnp.exp2(jnp.where(rowmod_h >= d, bh - bd, MASK_VALUE))
        o = o + _dot(p.astype(BF16), seg) * vd
        for n in range(d * per_step, min((d + 1) * per_step, nblk)):
            block_kv(n)
        for n in range((d - 1) * per_step, min(d * per_step, nblk)):
            advance(n)
        yield
    for n in range((half - 1) * per_step, nblk):
        advance(n)
    for h, (hs, lt) in enumerate(zip(head_rows, head_lanes)):
        s_ref[hs, lt] = sts[h]

    for h, c_out in enumerate(cross_out):
        hv = slice(h * HEAD_DIM, (h + 1) * HEAD_DIM)
        oacc_ref[:, hv] = oacc_ref[:, hv] + c_out
    return o + oacc_ref[...]


def _head_sum(x):
    gw = GROUP_WIDTH
    seg = (lax.broadcasted_iota(jnp.int32, (gw, gw), 0) // HEAD_DIM
           == lax.broadcasted_iota(jnp.int32, (gw, gw), 1) // HEAD_DIM).astype(BF16)
    hi = x.astype(BF16)
    lo = (x - hi.astype(F32)).astype(BF16)
    return _dot(hi, seg) + _dot(lo, seg)


def _head_norm_gate(o, ng, gate_act, o_ref):
    ms = _head_sum(o * o) * (1.0 / HEAD_DIM)
    o_ref[0] = (o * lax.rsqrt(ms + EPS) * ng * gate_act).astype(o_ref.dtype)


def _hgrn_body(z_ref, lb_ref, ng_ref, o_ref, s_ref, oacc_ref, kv_ref, sc_ref):
    gw = GROUP_WIDTH
    q = z_ref[0, :, 0:gw]
    zf = z_ref[0, :, gw:2 * gw]
    v = z_ref[0, :, 2 * gw:3 * gw]
    gate = z_ref[0, :, 3 * gw:4 * gw]
    lb = lb_ref[...]
    f = lb + (1.0 - lb) * _sigmoid(zf)
    lg = jnp.log(jnp.maximum(f, MIN_POS))
    k = (1.0 - lb) * _sigmoid(-zf)
    qh = q * _sigmoid(q) * HEAD_DIM ** -0.5
    yield
    o = yield from _gla_core(qh, k, v, lg, s_ref, oacc_ref, kv_ref, sc_ref, HEAD_DIM)
    yield
    _head_norm_gate(o, ng_ref[...], _sigmoid(gate), o_ref)


def _gla_body(z_ref, m_ref, wgk_ref, bgk_ref, ng_ref, o_ref, s_ref, oacc_ref, kv_ref, sc_ref):
    dk = N_HEADS * DK_B
    q = z_ref[0, :, 0:dk]
    k = z_ref[0, :, dk:2 * dk]
    v = z_ref[0, :, 2 * dk:2 * dk + GROUP_WIDTH]
    gate = z_ref[0, :, 2 * dk + GROUP_WIDTH:2 * dk + 2 * GROUP_WIDTH]
    x = _dot(m_ref[0].astype(BF16), wgk_ref[...]) + bgk_ref[...]
    lg = -_softplus(-x) / GLA_GATE_NORMALIZER
    yield
    o = yield from _gla_core(q * DK_B ** -0.5, k, v, lg, s_ref, oacc_ref, kv_ref, sc_ref, DK_B)
    yield
    _head_norm_gate(o, ng_ref[...], gate * _sigmoid(gate), o_ref)


def _row(a):
    return pl.BlockSpec(a.shape, lambda b, t: (0,) * a.ndim)


def _gla_scratch(tt, dk):
    nblk = tt // BAND
    return [pltpu.VMEM((GROUP_WIDTH, dk), F32),
            pltpu.VMEM((tt, GROUP_WIDTH), F32),
            pltpu.VMEM((nblk, GROUP_WIDTH, dk), F32),
            pltpu.VMEM((nblk, GROUP_WIDTH, dk), BF16)]


def _gdn_body(z_ref, m_ref, cw_ref, apad_ref, dtpad_ref, ebeta_ref, ea_ref, ng_ref,
              o_ref, xs_ref, s_ref, oacc_ref, u_ref, w_ref, aqk_ref, sc_ref):
    tt = z_ref.shape[1]
    gw = GROUP_WIDTH
    c = GDN_CHUNK
    halo = SUBLANE

    xs_ref[halo:halo + tt, :] = z_ref[0, :, 0:3 * gw]
    acc = cw_ref[GDN_CONV - 1:GDN_CONV, :] * xs_ref[halo:halo + tt, :]
    for j in range(1, GDN_CONV):
        acc = acc + cw_ref[GDN_CONV - 1 - j:GDN_CONV - j, :] * xs_ref[halo - j:halo - j + tt, :]
    xs_ref[0:halo, :] = xs_ref[tt:tt + halo, :]
    qkv = acc * _sigmoid(acc)
    yield

    q_all, k_all, v_all = qkv[:, 0:gw], qkv[:, gw:2 * gw], qkv[:, 2 * gw:3 * gw]

    head_blk = (lax.broadcasted_iota(jnp.int32, (gw, gw), 0) // HEAD_DIM
                == lax.broadcasted_iota(jnp.int32, (gw, gw), 1) // HEAD_DIM)
    def expand(x, e_ref, terms):
        acc, r = None, x
        for _ in range(terms):
            hi = r.astype(BF16)
            r = r - hi.astype(F32)
            part = _dot(hi, e_ref[...])
            acc = part if acc is None else acc + part
        return acc

    qn = q_all * lax.rsqrt(_head_sum(q_all * q_all) + EPS) * HEAD_DIM ** -0.5
    kn = k_all * lax.rsqrt(_head_sum(k_all * k_all) + EPS)
    yield

    m = m_ref[0]
    beta_x = expand(_sigmoid(m), ebeta_ref, 2)
    lg_m = -apad_ref[...] * _softplus(m + dtpad_ref[...])
    rowmod_m = lax.broadcasted_iota(jnp.int32, (tt, LANE), 0) & (c - 1)
    b_m = _block_cumsum(lg_m, rowmod_m, c)
    b_x = expand(b_m, ea_ref, 3)
    eb = jnp.exp(b_x)
    kb = kn * beta_x
    vb = v_all * beta_x
    kbe = kb * eb
    qe = qn * eb
    yield

    ii = lax.broadcasted_iota(jnp.int32, (tt, tt), 0)
    jj = lax.broadcasted_iota(jnp.int32, (tt, tt), 1)
    same = (ii // c) == (jj // c)
    incl = same & (ii >= jj)
    strict = same & (ii > jj)

    heads = [slice(h * HEAD_DIM, (h + 1) * HEAD_DIM) for h in range(N_HEADS)]
    nms = []
    for h, sl in enumerate(heads):
        bmat = jnp.broadcast_to(b_m[:, M_A + h:M_A + h + 1], (tt, tt))
        lmask = jnp.where(incl, jnp.exp(jnp.minimum(bmat - bmat.T, 0.0)), 0.0)
        kh16 = kn[:, sl].astype(BF16)
        nms.append(jnp.where(strict, -_dot(kb[:, sl].astype(BF16), kh16, NT) * lmask, 0.0))
        aqk_ref[h] = (_dot(qn[:, sl].astype(BF16), kh16, NT) * lmask).astype(BF16)
        if h % 2 == 1:
            yield
    tns = list(nms)
    pws16 = [p.astype(BF16) for p in nms]
    for _ in range(5):
        pws = [_dot(p, p) for p in pws16]
        pws16 = [p.astype(BF16) for p in pws]
        tns = [t + p + _dot(t.astype(BF16), p16) for t, p, p16 in zip(tns, pws, pws16)]
        yield
    for sl, tn in zip(heads, tns):
        tn16 = tn.astype(BF16)
        u_ref[:, sl] = vb[:, sl] + _dot(tn16, vb[:, sl].astype(BF16))
        w_ref[:, sl] = kbe[:, sl] + _dot(tn16, kbe[:, sl].astype(BF16))
    yield

    decays, pms, rms = [], [], []
    for n in range(tt // c):
        rs = slice(n * c, (n + 1) * c)
        b_last = b_x[n * c + c - 1:(n + 1) * c, :]
        kt_t = (kn[rs] * jnp.exp(b_last - b_x[rs])).T.astype(BF16)
        decays.append(jnp.exp(b_last))
        pms.append(jnp.where(head_blk, _dot(kt_t, w_ref[rs, :].astype(BF16)), 0.0).astype(BF16))
        rms.append(jnp.where(head_blk, _dot(kt_t, u_ref[rs, :].astype(BF16)), 0.0))
    yield
    st = s_ref[...]
    for n in range(tt // c):
        st16 = st.astype(BF16)
        sc_ref[n] = st16
        st = st * decays[n] - _dot(pms[n], st16) + rms[n]
    s_ref[...] = st
    yield

    for n in range(tt // c):
        rs = slice(n * c, (n + 1) * c)
        lhs = jnp.concatenate([w_ref[rs, :], qe[rs]], axis=0).astype(BF16)
        res = _dot(lhs, sc_ref[n])
        u_ref[rs, :] = u_ref[rs, :] - res[0:c]
        oacc_ref[rs, :] = res[c:2 * c]
    for h in range(N_HEADS):
        sl = slice(h * HEAD_DIM, (h + 1) * HEAD_DIM)
        oacc_ref[:, sl] = oacc_ref[:, sl] + _dot(aqk_ref[h], u_ref[:, sl].astype(BF16))
    yield

    zg = z_ref[0, :, 3 * gw:4 * gw]
    _head_norm_gate(oacc_ref[...], ng_ref[...], zg * _sigmoid(zg), o_ref)


def _interleave(*stages):
    live = list(stages)
    while live:
        for g in list(live):
            try:
                next(g)
            except StopIteration:
                live.remove(g)


N_HGRN_IN, N_GLA_IN, N_GDN_IN = 3, 5, 8
N_GLA_SCRATCH, N_GDN_SCRATCH = 4, 7
GDN_LEAD_STAGES = 6


def _mixers_kernel(*refs):
    it = iter(refs)
    take = lambda n: [next(it) for _ in range(n)]
    hgrn_in, gla_in, gdn_in = take(N_HGRN_IN), take(N_GLA_IN), take(N_GDN_IN)
    oa_ref, ob_ref, od_ref = take(3)
    hgrn_scr, gla_scr, gdn_scr = take(N_GLA_SCRATCH), take(N_GLA_SCRATCH), take(N_GDN_SCRATCH)

    @pl.when(pl.program_id(1) == 0)
    def _():
        for scr in (hgrn_scr, gla_scr):
            scr[0][...] = jnp.zeros_like(scr[0])
            scr[3][...] = jnp.zeros_like(scr[3])
        gdn_scr[1][...] = jnp.zeros_like(gdn_scr[1])
        gdn_scr[0][0:SUBLANE, :] = jnp.zeros((SUBLANE, gdn_scr[0].shape[1]), F32)

    gdn = _gdn_body(*gdn_in, od_ref, *gdn_scr)
    for _ in range(GDN_LEAD_STAGES):
        next(gdn)
    _interleave(gdn,
                _hgrn_body(*hgrn_in, oa_ref, *hgrn_scr),
                _gla_body(*gla_in, ob_ref, *gla_scr))


def _mixers(za, zb, zd, zm, hgrn_params, gla_params, gdn_params, tt):
    bsz, s, _ = za.shape
    gw = GROUP_WIDTH
    tok = lambda a: pl.BlockSpec((1, tt, a.shape[2]), lambda b, t: (b, t, 0))
    operands = [za, *hgrn_params, zb, zm, *gla_params, zd, zm, *gdn_params]
    in_specs = ([tok(za)] + [_row(a) for a in hgrn_params]
                + [tok(zb), tok(zm)] + [_row(a) for a in gla_params]
                + [tok(zd), tok(zm)] + [_row(a) for a in gdn_params])
    assert (len(hgrn_params) + 1, len(gla_params) + 2, len(gdn_params) + 2) == (N_HGRN_IN, N_GLA_IN, N_GDN_IN)
    out = pl.BlockSpec((1, tt, gw), lambda b, t: (b, t, 0))
    gdn_scratch = [pltpu.VMEM((tt + 2 * SUBLANE, 3 * gw), F32),
                   pltpu.VMEM((gw, gw), F32),
                   pltpu.VMEM((tt, gw), F32),
                   pltpu.VMEM((tt, gw), F32),
                   pltpu.VMEM((tt, gw), F32),
                   pltpu.VMEM((N_HEADS, tt, tt), BF16),
                   pltpu.VMEM((tt // GDN_CHUNK, gw, gw), BF16)]
    assert len(gdn_scratch) == N_GDN_SCRATCH and len(_gla_scratch(tt, gw)) == N_GLA_SCRATCH
    return pl.pallas_call(
        _mixers_kernel,
        grid=(bsz, s // tt),
        in_specs=in_specs,
        out_specs=[out, out, out],
        out_shape=[jax.ShapeDtypeStruct((bsz, s, gw), BF16)] * 3,
        scratch_shapes=_gla_scratch(tt, gw) + _gla_scratch(tt, N_HEADS * DK_B) + gdn_scratch,
        compiler_params=_cparams(("parallel", "arbitrary")),
        name="mixers",
    )(*operands)


def _mla_pre_kernel(zc_ref, m1_ref, m2_ref, gq_ref, gkv_ref, wqa_ref, wqb_ref, wk_ref, wv_ref,
                    cosq_ref, sinq_ref, cosk_ref, qt_ref, k_ref, vt_ref):
    scale = (MLA_NOPE + MLA_ROPE) ** -0.5 * LOG2E
    cq = _rms(zc_ref[0, :, 0:MLA_Q_RANK], gq_ref[...]).astype(BF16)
    ckv = _rms(zc_ref[0, :, MLA_Q_RANK:MLA_Q_RANK + MLA_KV_RANK], gkv_ref[...]).astype(BF16)
    qa = _dot(cq, wqa_ref[...])
    qb = _dot(cq, wqb_ref[...])
    kn = _dot(ckv, wk_ref[...])
    vv = _dot(ckv, wv_ref[...])
    cosq, sinq = cosq_ref[...], sinq_ref[...]
    kr = m1_ref[0] * cosk_ref[...] + m2_ref[0] * sinq
    for h in range(N_HEADS):
        sl = slice(h * LANE, (h + 1) * LANE)
        q = (qa[:, sl] * cosq + qb[:, sl] * sinq) * scale
        qt_ref[0, sl, :] = q.T.astype(BF16)
        k_ref[0, h] = (kn[:, sl] + kr).astype(BF16)
    vvt = vv.T.astype(BF16)
    for h in range(N_HEADS):
        vt_ref[0, h * V_ROWS:h * V_ROWS + HEAD_DIM, :] = vvt[h * HEAD_DIM:(h + 1) * HEAD_DIM, :]
        vt_ref[0, h * V_ROWS + HEAD_DIM:(h + 1) * V_ROWS, :] = jnp.ones((V_ROWS - HEAD_DIM, vvt.shape[1]), BF16)


def _mla_pre(zc, zm, zm2, gq, gkv, wqa, wqb, wk, wv, cosq, sinq, cosk, tm):
    bsz, s, _ = zc.shape
    tok = lambda wd: pl.BlockSpec((1, tm, wd), lambda b, t: (b, t, 0))
    tab = pl.BlockSpec((tm, LANE), lambda b, t: (t, 0))
    tr = lambda rows: pl.BlockSpec((1, rows, tm), lambda b, t: (b, 0, t))
    return pl.pallas_call(
        _mla_pre_kernel,
        grid=(bsz, s // tm),
        in_specs=[tok(zc.shape[2]), tok(LANE), tok(LANE), _row(gq), _row(gkv),
                  _row(wqa), _row(wqb), _row(wk), _row(wv), tab, tab, tab],
        out_specs=[tr(N_HEADS * LANE),
                   pl.BlockSpec((1, N_HEADS, tm, LANE), lambda b, t: (b, 0, t, 0)),
                   tr(N_HEADS * V_ROWS)],
        out_shape=[jax.ShapeDtypeStruct((bsz, N_HEADS * LANE, s), BF16),
                   jax.ShapeDtypeStruct((bsz, N_HEADS, s, LANE), BF16),
                   jax.ShapeDtypeStruct((bsz, N_HEADS * V_ROWS, s), BF16)],
        compiler_params=_cparams(("parallel", "parallel")),
        name="mla_pre",
    )(zc, zm, zm2, gq, gkv, wqa, wqb, wk, wv, cosq, sinq, cosk)


def _mla_attn_kernel(qi_ref, kj_ref, qt_ref, k_ref, vt_ref, o_ref, m_ref, acc_ref):
    pair = pl.program_id(1)
    i = qi_ref[pair]
    j = kj_ref[pair]
    tq = qt_ref.shape[2]
    tk = k_ref.shape[2]

    @pl.when(j == 0)
    def _():
        m_ref[...] = jnp.full_like(m_ref, MASK_VALUE)
        acc_ref[...] = jnp.zeros_like(acc_ref)

    def sub_step(k0, q0, masked):
        nk = tk // 2
        ks = slice(k0, k0 + nk)
        qs = slice(q0, tq)
        scores = [_dot(k_ref[0, h, ks, :], qt_ref[0, h * LANE:(h + 1) * LANE, qs]) for h in range(N_HEADS)]
        if masked:
            keep = (lax.broadcasted_iota(jnp.int32, (nk, tq - q0), 0) + k0
                    <= lax.broadcasted_iota(jnp.int32, (nk, tq - q0), 1) + q0)
            scores = [jnp.where(keep, s, MASK_VALUE) for s in scores]
        m_prevs = [m_ref[h:h + 1, qs] for h in range(N_HEADS)]
        m_news = [jnp.maximum(mp, jnp.max(s, axis=0, keepdims=True)) for mp, s in zip(m_prevs, scores)]
        ps = [jnp.exp2(s - mn) for s, mn in zip(scores, m_news)]
        for h in range(N_HEADS):
            alpha = jnp.exp2(m_prevs[h] - m_news[h])
            hs = slice(h * V_ROWS, (h + 1) * V_ROWS)
            acc_ref[hs, qs] = alpha * acc_ref[hs, qs] + _dot(vt_ref[0, hs, ks], ps[h].astype(BF16))
            m_ref[h:h + 1, qs] = m_news[h]

    @pl.when(j < i)
    def _():
        sub_step(0, 0, False)
        sub_step(tk // 2, 0, False)

    @pl.when(j == i)
    def _():
        sub_step(0, 0, True)
        sub_step(tk // 2, tq // 2, True)
        outs = [acc_ref[h * V_ROWS:h * V_ROWS + HEAD_DIM, :] / acc_ref[h * V_ROWS + HEAD_DIM:h * V_ROWS + HEAD_DIM + 1, :]
                for h in range(N_HEADS)]
        o_ref[0] = jnp.concatenate(outs, axis=0).T.astype(o_ref.dtype)


def _mla_attn(qt, k, vt, tq):
    bsz, _, s = qt.shape
    nq = s // tq
    pairs = [(i, j) for i in range(nq) for j in range(i + 1)]
    qi = jnp.array([p[0] for p in pairs], jnp.int32)
    kj = jnp.array([p[1] for p in pairs], jnp.int32)
    grid_spec = pltpu.PrefetchScalarGridSpec(
        num_scalar_prefetch=2,
        grid=(bsz, len(pairs)),
        in_specs=[pl.BlockSpec((1, N_HEADS * LANE, tq), lambda b, p, qi, kj: (b, 0, qi[p])),
                  pl.BlockSpec((1, N_HEADS, tq, LANE), lambda b, p, qi, kj: (b, 0, kj[p], 0)),
                  pl.BlockSpec((1, N_HEADS * V_ROWS, tq), lambda b, p, qi, kj: (b, 0, kj[p]))],
        out_specs=pl.BlockSpec((1, tq, GROUP_WIDTH), lambda b, p, qi, kj: (b, qi[p], 0)),
        scratch_shapes=[pltpu.VMEM((SUBLANE, tq), F32), pltpu.VMEM((N_HEADS * V_ROWS, tq), F32)],
    )
    return pl.pallas_call(
        _mla_attn_kernel,
        grid_spec=grid_spec,
        out_shape=jax.ShapeDtypeStruct((bsz, s, GROUP_WIDTH), BF16),
        compiler_params=_cparams(("parallel", "arbitrary")),
        name="mla_attn",
    )(qi, kj, qt, k, vt)


def _gelu_tanh(x):
    return 0.5 * x * (1.0 + jnp.tanh(0.7978845608028654 * (x + 0.044715 * (x * x * x))))


def _ffn_kernel(oa_ref, ob_ref, oc_ref, od_ref, wo_ref, g0_ref, x_ref,
                g1_ref, wg_ref, wu_ref, cw_ref, wd_ref, g2_ref, y_ref, prev_ref, act_ref, *, fb):
    tm = x_ref.shape[1]
    f = wg_ref.shape[1]

    @pl.when(pl.program_id(1) == 0)
    def _():
        prev_ref[...] = jnp.zeros_like(prev_ref)

    mix = jnp.concatenate([oa_ref[0], ob_ref[0], oc_ref[0], od_ref[0]], axis=1)
    x = x_ref[0] + _rms(_dot(mix, wo_ref[...]), g0_ref[...])
    h = _rms(x, g1_ref[...]).astype(BF16)
    row = lax.broadcasted_iota(jnp.int32, (tm, fb), 0)
    for n in range(f // fb):
        cs = slice(n * fb, (n + 1) * fb)
        gate = _dot(h, wg_ref[:, cs])
        up = _dot(h, wu_ref[:, cs])
        p1 = prev_ref[1:2, cs]
        p2 = prev_ref[0:1, cs]
        g1 = jnp.where(row == 0, p1, pltpu.roll(gate, 1, 0))
        g2 = jnp.where(row == 0, p2, jnp.where(row == 1, p1, pltpu.roll(gate, 2, 0)))
        prev_ref[0:2, cs] = gate[tm - 2:tm, :]
        conv = cw_ref[2:3, cs] * gate + cw_ref[1:2, cs] * g1 + cw_ref[0:1, cs] * g2
        act_ref[:, cs] = (_gelu_tanh(conv) * up).astype(BF16)
    y_ref[0] = x + _rms(_dot(act_ref[...], wd_ref[...]), g2_ref[...])


def _ffn(outs, wo, g0, x, g1, wg, wu, cw, wd, g2, layer, tm, fb):
    bsz, s, d = x.shape
    grp = pl.BlockSpec((1, tm, GROUP_WIDTH), lambda b, t: (b, t, 0))
    once = lambda a: pl.BlockSpec((None,) + a.shape[1:], lambda b, t: (layer, 0, 0), pipeline_mode=pl.Buffered(1))
    return pl.pallas_call(
        functools.partial(_ffn_kernel, fb=fb),
        grid=(bsz, s // tm),
        in_specs=[grp, grp, grp, grp, once(wo), _row(g0),
                  pl.BlockSpec((1, tm, d), lambda b, t: (b, t, 0)),
                  _row(g1), once(wg), once(wu), _row(cw), once(wd), _row(g2)],
        out_specs=pl.BlockSpec((1, tm, d), lambda b, t: (b, t, 0)),
        out_shape=jax.ShapeDtypeStruct((bsz, s, d), F32),
        scratch_shapes=[pltpu.VMEM((SUBLANE, wg.shape[2]), F32),
                        pltpu.VMEM((tm, wg.shape[2]), BF16)],
        compiler_params=_cparams(("parallel", "arbitrary")),
        name="ffn",
    )(*outs, wo, g0, x, g1, wg, wu, cw, wd, g2)


def _cat(pieces, like):
    lead = like.shape[:-1]
    return jnp.concatenate([jnp.zeros(lead + (p,), like.dtype) if isinstance(p, int) else p for p in pieces], axis=-1)


def _rot_cols(w):
    half = w.shape[-1] // 2
    return jnp.concatenate([-w[..., half:], w[..., :half]], axis=-1)


def _in_proj_layout(w):
    gw, dkb = GROUP_WIDTH, N_HEADS * DK_B
    o_b = 4 * gw
    o_code = o_b + 2 * dkb + gw
    o_bg = o_code + GLA_GATE_RANK
    o_c = o_bg + gw
    o_kr = o_c + MLA_Q_RANK + MLA_KV_RANK
    o_d = o_kr + MLA_ROPE
    o_beta = o_d + 3 * gw
    o_a = o_beta + N_HEADS
    o_z = o_a + N_HEADS
    assert o_z + gw == w.shape[-1]
    assert (M_CODE, M_BETA, M_A) == (0, GLA_GATE_RANK, GLA_GATE_RANK + N_HEADS)
    kr = w[..., o_kr:o_d]
    misc_gap = M_KR - (M_A + N_HEADS)
    tail_gap = LANE - M_KR - MLA_ROPE
    cols = [w[..., 0:o_b],
            w[..., o_b:o_code], w[..., o_bg:o_c],
            w[..., o_c:o_kr],
            w[..., o_d:o_beta], w[..., o_z:],
            w[..., o_code:o_bg], w[..., o_beta:o_a], w[..., o_a:o_z], misc_gap, kr, tail_gap,
            M_KR, _rot_cols(kr), tail_gap]
    widths = (4 * gw, 2 * dkb + 2 * gw, MLA_Q_RANK + MLA_KV_RANK, 4 * gw, LANE, LANE)
    return _cat(cols, w).astype(BF16), widths


def _mla_layout(w_uq, w_ukv):
    dq = MLA_NOPE + MLA_ROPE
    pad = LANE - dq
    qa, qb, wk, wv = [], [], [], []
    for h in range(N_HEADS):
        nope = w_uq[..., h * dq:h * dq + MLA_NOPE]
        rope = w_uq[..., h * dq + MLA_NOPE:(h + 1) * dq]
        qa += [nope, rope, pad]
        qb += [MLA_NOPE, _rot_cols(rope), pad]
        wk += [w_ukv[..., h * 2 * HEAD_DIM:h * 2 * HEAD_DIM + MLA_NOPE], LANE - MLA_NOPE]
        wv += [w_ukv[..., h * 2 * HEAD_DIM + MLA_NOPE:(h + 1) * 2 * HEAD_DIM]]
    return tuple(_cat(p, like).astype(BF16) for p, like in ((qa, w_uq), (qb, w_uq), (wk, w_ukv), (wv, w_ukv)))


def _rope_tables(s):
    inv = ROPE_THETA ** (-jnp.arange(0, MLA_ROPE, 2, dtype=F32) / MLA_ROPE)
    ang = jnp.arange(s, dtype=F32)[:, None] * inv[None, :]
    cos, sin = jnp.cos(ang), jnp.sin(ang)
    pad = LANE - MLA_NOPE - MLA_ROPE
    cosq = _cat([jnp.ones((s, MLA_NOPE), F32), cos, cos, pad], cos)
    sinq = _cat([MLA_NOPE, sin, sin, pad], cos)
    cosk = _cat([MLA_NOPE, cos, cos, pad], cos)
    return cosq, sinq, cosk


def _tile_heads(g):
    return jnp.tile(g, N_HEADS)[None, :]


def _pick(n, prefs):
    for p in prefs:
        if n % p == 0:
            return p
    return n


def kernel(x, w_in, w_out, pre_mix_g, post_mix_g, pre_ffn_g, post_ffn_g, hgrn_lb_logits, hgrn_norm_g,
           gla_w_gk2, gla_b_gk, gla_norm_g, mla_q_norm_g, mla_w_uq, mla_kv_norm_g, mla_w_ukv,
           gdn_conv_w, gdn_a_log, gdn_dt_bias, gdn_norm_g, ffn_w_gate, ffn_w_up, ffn_conv_w, ffn_w_down):
    bsz, s, d = x.shape
    depth = w_in.shape[0]
    t = bsz * s
    tm = _pick(t, (512, 256))
    tt = _pick(s, (256,))
    ts = _pick(s, (512, 256))
    tq = _pick(s, (1024, 512, 256, 128))
    fb = 256

    cosq, sinq, cosk = _rope_tables(s)
    p = jax.nn.softmax(hgrn_lb_logits.astype(F32), axis=0)
    lower_bounds = jnp.cumsum(p, axis=0) - p[0]

    lanes = jnp.arange(LANE)
    head_of = jnp.arange(GROUP_WIDTH) // HEAD_DIM
    ebeta = (lanes[:, None] == M_BETA + head_of[None, :]).astype(BF16)
    ea = (lanes[:, None] == M_A + head_of[None, :]).astype(BF16)

    w_in_all, widths = _in_proj_layout(w_in)
    wqa, wqb, wk, wv = _mla_layout(mla_w_uq, mla_w_ukv)
    wgk = jnp.concatenate([gla_w_gk2, jnp.zeros((depth, LANE - GLA_GATE_RANK, N_HEADS * DK_B), F32)],
                          axis=1).astype(BF16)
    a_tail = LANE - M_A - N_HEADS
    apad = _cat([M_A, jnp.exp(gdn_a_log.astype(F32)), a_tail], gdn_a_log)
    dtpad = _cat([M_A, gdn_dt_bias, a_tail], gdn_dt_bias)
    w_out16, wg16, wu16, wd16 = (a.astype(BF16) for a in (w_out, ffn_w_gate, ffn_w_up, ffn_w_down))

    for l in range(depth):
        za, zb, zc, zd, zm, zm2 = _in_proj(x.reshape(t, d), pre_mix_g[l][None, :], w_in_all, l, widths, tm)
        r3 = lambda a: a.reshape(bsz, s, a.shape[-1])
        za, zb, zc, zd, zm, zm2 = map(r3, (za, zb, zc, zd, zm, zm2))

        qt, kh, vt = _mla_pre(zc, zm, zm2, mla_q_norm_g[l][None, :], mla_kv_norm_g[l][None, :],
                              wqa[l], wqb[l], wk[l], wv[l], cosq, sinq, cosk, tq)
        o_c = _mla_attn(qt, kh, vt, tq)
        o_a, o_b, o_d = _mixers(
            za, zb, zd, zm,
            (lower_bounds[l][None, :], _tile_heads(hgrn_norm_g[l])),
            (wgk[l], gla_b_gk[l][None, :], _tile_heads(gla_norm_g[l])),
            (gdn_conv_w[l], apad[l][None, :], dtpad[l][None, :], ebeta, ea, _tile_heads(gdn_norm_g[l])), tt)

        x = _ffn((o_a, o_b, o_c, o_d), w_out16, post_mix_g[l][None, :], x,
                 pre_ffn_g[l][None, :], wg16, wu16, ffn_conv_w[l], wd16, post_ffn_g[l][None, :],
                 layer=l, tm=ts, fb=fb)
    return x
```

```python
import functools

import jax
import jax.numpy as jnp
from jax import lax
from jax.experimental import pallas as pl
from jax.experimental.pallas import tpu as pltpu

F32 = jnp.float32
BF16 = jnp.bfloat16
HIGHEST = lax.Precision.HIGHEST

N_GROUPS = 4
HEAD_DIM = 64
N_HEADS = 4
GROUP_WIDTH = N_HEADS * HEAD_DIM
EPS = 1e-6
MASK_VALUE = -1e30
MIN_POS = 1e-30
LOG2E = 1.4426950408889634

DK_B = HEAD_DIM // 2
GLA_GATE_RANK = 16
GLA_GATE_NORMALIZER = 16.0
MLA_Q_RANK = 256
MLA_KV_RANK = 128
MLA_NOPE = 64
MLA_ROPE = 32
ROPE_THETA = 10000.0
GDN_CONV = 4
FFN_CONV = 3
LANE = 128
SUBLANE = 8

M_CODE = 0
M_BETA = 16
M_A = 20
M_KR = 64

BAND = 16
GDN_CHUNK = 64
KEY_SPLIT = 4
V_ROWS = HEAD_DIM + 16
VMEM_LIMIT = 56 * 1024 * 1024


def _cparams(sem):
    return pltpu.CompilerParams(dimension_semantics=sem, vmem_limit_bytes=VMEM_LIMIT)


def _rms(x, g):
    return x * lax.rsqrt(jnp.mean(x * x, axis=-1, keepdims=True) + EPS) * g


def _sigmoid(x):
    return jax.nn.sigmoid(x)


def _softplus(x):
    return jnp.maximum(x, 0.0) + jnp.log1p(jnp.exp(-jnp.abs(x)))


def _block_cumsum(x, rowmod, block):
    s = 1
    while s < block:
        x = x + jnp.where(rowmod >= s, pltpu.roll(x, s, 0), 0.0)
        s *= 2
    return x


def _dot(a, b, dims=(((1,), (0,)), ((), ())), precision=None):
    return lax.dot_general(a, b, dims, precision=precision, preferred_element_type=F32)


NT = (((1,), (1,)), ((), ()))
TN = (((0,), (0,)), ((), ()))


def _in_proj_kernel(x_ref, g_ref, w_ref, *out_refs):
    h = _rms(x_ref[...], g_ref[...]).astype(BF16)
    z = _dot(h, w_ref[...])
    off = 0
    for o_ref in out_refs:
        n = o_ref.shape[-1]
        o_ref[...] = z[:, off:off + n]
        off += n


def _in_proj(x2, g, w_all, layer, widths, tm):
    t, d = x2.shape
    n = w_all.shape[2]
    return pl.pallas_call(
        _in_proj_kernel,
        grid=(t // tm,),
        in_specs=[pl.BlockSpec((tm, d), lambda i: (i, 0)),
                  pl.BlockSpec((1, d), lambda i: (0, 0)),
                  pl.BlockSpec((None, d, n), lambda i: (layer, 0, 0))],
        out_specs=[pl.BlockSpec((tm, wd), lambda i: (i, 0)) for wd in widths],
        out_shape=[jax.ShapeDtypeStruct((t, wd), F32) for wd in widths],
        compiler_params=_cparams(("parallel",)),
        name="in_proj",
    )(x2, g, w_all)


def _gla_core(q, k, v, lg, seg, same_blk, s_ref, oacc_ref, kv_ref, sc_ref, dkh):
    tt, dk = q.shape
    dv = v.shape[1]
    half = BAND // 2
    nblk = tt // BAND
    rowmod = lax.broadcasted_iota(jnp.int32, (tt, dk), 0) & (BAND - 1)
    b = _block_cumsum(lg * LOG2E, rowmod, BAND)
    upper = rowmod >= half
    mid = jnp.broadcast_to(b.reshape(nblk, BAND, dk)[:, half - 1:half, :], (nblk, BAND, dk)).reshape(tt, dk)
    bh = jnp.where(upper, b - mid, b)
    rowmod_h = rowmod & (half - 1)
    yield

    def roll_half(x, d):
        return pltpu.roll(x.reshape(tt // half, half, x.shape[1]), d, 1).reshape(tt, x.shape[1])

    def block_kv(n):
        r0 = n * BAND
        bn = b[r0:r0 + BAND]
        kt = (k[r0:r0 + BAND] * jnp.exp2(bn[BAND - 1:BAND] - bn)).astype(BF16)
        kv_ref[n] = _dot(v[r0:r0 + BAND].astype(BF16), kt, TN)

    qe = (q * jnp.exp2(b)).astype(BF16)
    head_rows = [slice(h * HEAD_DIM, (h + 1) * HEAD_DIM) for h in range(N_HEADS)]
    head_lanes = [slice((h * dkh // LANE) * LANE, (h * dkh // LANE) * LANE + LANE) for h in range(N_HEADS)]
    own = [(lax.broadcasted_iota(jnp.int32, (HEAD_DIM, LANE), 1) + lt.start) // dkh == h
           for h, lt in enumerate(head_lanes)]
    sts = [s_ref[hs, lt] for hs, lt in zip(head_rows, head_lanes)]

    def advance(n):
        r0 = n * BAND
        for h, (hs, lt) in enumerate(zip(head_rows, head_lanes)):
            sc_ref[n, hs, lt] = jnp.where(own[h], sts[h], 0.0).astype(BF16)
            sts[h] = sts[h] * jnp.exp2(b[r0 + BAND - 1:r0 + BAND, lt]) + kv_ref[n, hs, lt]
        oacc_ref[r0:r0 + BAND, :] = _dot(qe[r0:r0 + BAND], sc_ref[n], NT)

    q_up = (q * jnp.exp2(jnp.where(upper, bh, MASK_VALUE))).astype(BF16)
    k_lo = (k * jnp.exp2(jnp.where(upper, MASK_VALUE, mid - b))).astype(BF16)
    v16 = v.astype(BF16)
    cross = [_dot(q_up[:, h * dkh:(h + 1) * dkh], k_lo[:, h * dkh:(h + 1) * dkh], NT) * same_blk
             for h in range(N_HEADS)]
    cross_out = [_dot(a.astype(BF16), v16[:, h * HEAD_DIM:(h + 1) * HEAD_DIM]) for h, a in enumerate(cross)]
    yield

    per_step = -(-nblk // half)
    o = _dot((q * k).astype(BF16), seg) * v
    for n in range(per_step):
        block_kv(n)
    for d in range(1, half):
        kd = roll_half(k, d)
        bd = roll_half(bh, d)
        vd = roll_half(v, d)
        p = q * kd * jnp.exp2(jnp.where(rowmod_h >= d, bh - bd, MASK_VALUE))
        o = o + _dot(p.astype(BF16), seg) * vd
        for n in range(d * per_step, min((d + 1) * per_step, nblk)):
            block_kv(n)
        for n in range((d - 1) * per_step, min(d * per_step, nblk)):
            advance(n)
        yield
    for n in range((half - 1) * per_step, nblk):
        advance(n)
    for h, (hs, lt) in enumerate(zip(head_rows, head_lanes)):
        s_ref[hs, lt] = sts[h]

    for h, c_out in enumerate(cross_out):
        hv = slice(h * HEAD_DIM, (h + 1) * HEAD_DIM)
        oacc_ref[:, hv] = oacc_ref[:, hv] + c_out
    return o + oacc_ref[...]


def _head_sum(x, seg):
    hi = x.astype(BF16)
    lo = (x - hi.astype(F32)).astype(BF16)
    return _dot(hi, seg) + _dot(lo, seg)


def _head_norm_gate(o, ng, gate_act, seg, o_ref):
    ms = _head_sum(o * o, seg) * (1.0 / HEAD_DIM)
    o_ref[0] = (o * lax.rsqrt(ms + EPS) * ng * gate_act).astype(o_ref.dtype)


def _hgrn_body(z_ref, lb_ref, ng_ref, seg_ref, blk_ref, o_ref, s_ref, oacc_ref, kv_ref, sc_ref):
    gw = GROUP_WIDTH
    q = z_ref[0, :, 0:gw]
    zf = z_ref[0, :, gw:2 * gw]
    v = z_ref[0, :, 2 * gw:3 * gw]
    gate = z_ref[0, :, 3 * gw:4 * gw]
    lb = lb_ref[...]
    f = lb + (1.0 - lb) * _sigmoid(zf)
    lg = jnp.log(jnp.maximum(f, MIN_POS))
    k = (1.0 - lb) * _sigmoid(-zf)
    qh = q * _sigmoid(q) * HEAD_DIM ** -0.5
    yield
    o = yield from _gla_core(qh, k, v, lg, seg_ref[...], blk_ref[...], s_ref, oacc_ref, kv_ref, sc_ref, HEAD_DIM)
    yield
    _head_norm_gate(o, ng_ref[...], _sigmoid(gate), seg_ref[...], o_ref)


def _gla_body(z_ref, m_ref, wgk_ref, bgk_ref, ng_ref, segk_ref, seg_ref, blk_ref, o_ref, s_ref, oacc_ref, kv_ref, sc_ref):
    dk = N_HEADS * DK_B
    q = z_ref[0, :, 0:dk]
    k = z_ref[0, :, dk:2 * dk]
    v = z_ref[0, :, 2 * dk:2 * dk + GROUP_WIDTH]
    gate = z_ref[0, :, 2 * dk + GROUP_WIDTH:2 * dk + 2 * GROUP_WIDTH]
    x = _dot(m_ref[0].astype(BF16), wgk_ref[...]) + bgk_ref[...]
    lg = -_softplus(-x) / GLA_GATE_NORMALIZER
    yield
    o = yield from _gla_core(q * DK_B ** -0.5, k, v, lg, segk_ref[...], blk_ref[...], s_ref, oacc_ref, kv_ref, sc_ref,
                             DK_B)
    yield
    _head_norm_gate(o, ng_ref[...], gate * _sigmoid(gate), seg_ref[...], o_ref)


def _row(a):
    return pl.BlockSpec(a.shape, lambda b, t: (0,) * a.ndim)


def _gla_scratch(tt, dk):
    nblk = tt // BAND
    return [pltpu.VMEM((GROUP_WIDTH, dk), F32),
            pltpu.VMEM((tt, GROUP_WIDTH), F32),
            pltpu.VMEM((nblk, GROUP_WIDTH, dk), F32),
            pltpu.VMEM((nblk, GROUP_WIDTH, dk), BF16)]


def _gdn_body(z_ref, m_ref, cw_ref, apad_ref, dtpad_ref, ebeta_ref, ea_ref, ng_ref, seg_ref, headf_ref,
              incl_ref, nstrict_ref, o_ref, xs_ref, s_ref, oacc_ref, u_ref, w_ref, aqk_ref, sc_ref):
    tt = z_ref.shape[1]
    gw = GROUP_WIDTH
    c = GDN_CHUNK
    halo = SUBLANE

    xs_ref[halo:halo + tt, :] = z_ref[0, :, 0:3 * gw]
    acc = cw_ref[GDN_CONV - 1:GDN_CONV, :] * xs_ref[halo:halo + tt, :]
    for j in range(1, GDN_CONV):
        acc = acc + cw_ref[GDN_CONV - 1 - j:GDN_CONV - j, :] * xs_ref[halo - j:halo - j + tt, :]
    xs_ref[0:halo, :] = xs_ref[tt:tt + halo, :]
    qkv = acc * _sigmoid(acc)
    yield

    q_all, k_all, v_all = qkv[:, 0:gw], qkv[:, gw:2 * gw], qkv[:, 2 * gw:3 * gw]

    seg = seg_ref[...]
    head_f = headf_ref[...]
    def expand(x, e_ref, terms):
        acc, r = None, x
        for _ in range(terms):
            hi = r.astype(BF16)
            r = r - hi.astype(F32)
            part = _dot(hi, e_ref[...])
            acc = part if acc is None else acc + part
        return acc

    qn = q_all * lax.rsqrt(_head_sum(q_all * q_all, seg) + EPS) * HEAD_DIM ** -0.5
    kn = k_all * lax.rsqrt(_head_sum(k_all * k_all, seg) + EPS)
    yield

    m = m_ref[0]
    beta_x = expand(_sigmoid(m), ebeta_ref, 2)
    lg_m = -apad_ref[...] * _softplus(m + dtpad_ref[...])
    rowmod_m = lax.broadcasted_iota(jnp.int32, (tt, LANE), 0) & (c - 1)
    b_m = _block_cumsum(lg_m, rowmod_m, c)
    b_x = expand(b_m, ea_ref, 3)
    eb = jnp.exp(b_x)
    kb = kn * beta_x
    vb = v_all * beta_x
    kbe = kb * eb
    qe = qn * eb
    yield

    incl = incl_ref[...]
    nstrict = nstrict_ref[...]

    heads = [slice(h * HEAD_DIM, (h + 1) * HEAD_DIM) for h in range(N_HEADS)]
    nms = []
    for h, sl in enumerate(heads):
        bmat = jnp.broadcast_to(b_m[:, M_A + h:M_A + h + 1], (tt, tt))
        decay = jnp.exp(jnp.minimum(bmat - bmat.T, 0.0))
        kh16 = kn[:, sl].astype(BF16)
        nms.append(_dot(kb[:, sl].astype(BF16), kh16, NT) * (decay * nstrict))
        aqk_ref[h] = (_dot(qn[:, sl].astype(BF16), kh16, NT) * (decay * incl)).astype(BF16)
        if h % 2 == 1:
            yield
    tns = list(nms)
    pws16 = [p.astype(BF16) for p in nms]
    for _ in range(5):
        pws = [_dot(p, p) for p in pws16]
        pws16 = [p.astype(BF16) for p in pws]
        tns = [t + p + _dot(t.astype(BF16), p16) for t, p, p16 in zip(tns, pws, pws16)]
        yield
    for sl, tn in zip(heads, tns):
        tn16 = tn.astype(BF16)
        u_ref[:, sl] = vb[:, sl] + _dot(tn16, vb[:, sl].astype(BF16))
        w_ref[:, sl] = kbe[:, sl] + _dot(tn16, kbe[:, sl].astype(BF16))
    yield

    decays, pms, rms = [], [], []
    for n in range(tt // c):
        rs = slice(n * c, (n + 1) * c)
        b_last = b_x[n * c + c - 1:(n + 1) * c, :]
        kt_t = (kn[rs] * jnp.exp(b_last - b_x[rs])).T.astype(BF16)
        decays.append(jnp.exp(b_last))
        pms.append((_dot(kt_t, w_ref[rs, :].astype(BF16)) * head_f).astype(BF16))
        rms.append(_dot(kt_t, u_ref[rs, :].astype(BF16)) * head_f)
    yield
    st = s_ref[...]
    for n in range(tt // c):
        st16 = st.astype(BF16)
        sc_ref[n] = st16
        st = st * decays[n] - _dot(pms[n], st16) + rms[n]
    s_ref[...] = st
    yield

    for n in range(tt // c):
        rs = slice(n * c, (n + 1) * c)
        lhs = jnp.concatenate([w_ref[rs, :], qe[rs]], axis=0).astype(BF16)
        res = _dot(lhs, sc_ref[n])
        u_ref[rs, :] = u_ref[rs, :] - res[0:c]
        oacc_ref[rs, :] = res[c:2 * c]
    for h in range(N_HEADS):
        sl = slice(h * HEAD_DIM, (h + 1) * HEAD_DIM)
        oacc_ref[:, sl] = oacc_ref[:, sl] + _dot(aqk_ref[h], u_ref[:, sl].astype(BF16))
    yield

    zg = z_ref[0, :, 3 * gw:4 * gw]
    _head_norm_gate(oacc_ref[...], ng_ref[...], zg * _sigmoid(zg), seg, o_ref)


def _interleave(*stages):
    live = list(stages)
    while live:
        for g in list(live):
            try:
                next(g)
            except StopIteration:
                live.remove(g)


N_HGRN_IN, N_GLA_IN, N_GDN_IN = 5, 8, 12
N_GLA_SCRATCH, N_GDN_SCRATCH = 4, 7
GDN_LEAD_STAGES = 6


def _mixers_kernel(*refs):
    it = iter(refs)
    take = lambda n: [next(it) for _ in range(n)]
    hgrn_in, gla_in, gdn_in = take(N_HGRN_IN), take(N_GLA_IN), take(N_GDN_IN)
    oa_ref, ob_ref, od_ref = take(3)
    hgrn_scr, gla_scr, gdn_scr = take(N_GLA_SCRATCH), take(N_GLA_SCRATCH), take(N_GDN_SCRATCH)

    @pl.when(pl.program_id(1) == 0)
    def _():
        for scr in (hgrn_scr, gla_scr):
            scr[0][...] = jnp.zeros_like(scr[0])
            scr[3][...] = jnp.zeros_like(scr[3])
        gdn_scr[1][...] = jnp.zeros_like(gdn_scr[1])
        gdn_scr[0][0:SUBLANE, :] = jnp.zeros((SUBLANE, gdn_scr[0].shape[1]), F32)

    gdn = _gdn_body(*gdn_in, od_ref, *gdn_scr)
    for _ in range(GDN_LEAD_STAGES):
        next(gdn)
    _interleave(gdn,
                _hgrn_body(*hgrn_in, oa_ref, *hgrn_scr),
                _gla_body(*gla_in, ob_ref, *gla_scr))


def _mixers(za, zb, zd, zm, hgrn_params, gla_params, gdn_params, tt):
    bsz, s, _ = za.shape
    gw = GROUP_WIDTH
    tok = lambda a: pl.BlockSpec((1, tt, a.shape[2]), lambda b, t: (b, t, 0))
    rows = jnp.arange(tt)
    same_band = (rows[:, None] // BAND == rows[None, :] // BAND).astype(F32)
    same_chunk = rows[:, None] // GDN_CHUNK == rows[None, :] // GDN_CHUNK
    incl = (same_chunk & (rows[:, None] >= rows[None, :])).astype(F32)
    nstrict = -(same_chunk & (rows[:, None] > rows[None, :])).astype(F32)
    lane_head = jnp.arange(gw) // HEAD_DIM
    head_f = (lane_head[:, None] == lane_head[None, :]).astype(F32)
    seg = head_f.astype(BF16)
    seg_k = (jnp.arange(N_HEADS * DK_B)[:, None] // DK_B == lane_head[None, :]).astype(BF16)
    hgrn_params = (*hgrn_params, seg, same_band)
    gla_params = (*gla_params, seg_k, seg, same_band)
    gdn_params = (*gdn_params, seg, head_f, incl, nstrict)
    operands = [za, *hgrn_params, zb, zm, *gla_params, zd, zm, *gdn_params]
    in_specs = ([tok(za)] + [_row(a) for a in hgrn_params]
                + [tok(zb), tok(zm)] + [_row(a) for a in gla_params]
                + [tok(zd), tok(zm)] + [_row(a) for a in gdn_params])
    assert (len(hgrn_params) + 1, len(gla_params) + 2, len(gdn_params) + 2) == (N_HGRN_IN, N_GLA_IN, N_GDN_IN)
    out = pl.BlockSpec((1, tt, gw), lambda b, t: (b, t, 0))
    gdn_scratch = [pltpu.VMEM((tt + 2 * SUBLANE, 3 * gw), F32),
                   pltpu.VMEM((gw, gw), F32),
                   pltpu.VMEM((tt, gw), F32),
                   pltpu.VMEM((tt, gw), F32),
                   pltpu.VMEM((tt, gw), F32),
                   pltpu.VMEM((N_HEADS, tt, tt), BF16),
                   pltpu.VMEM((tt // GDN_CHUNK, gw, gw), BF16)]
    assert len(gdn_scratch) == N_GDN_SCRATCH and len(_gla_scratch(tt, gw)) == N_GLA_SCRATCH
    return pl.pallas_call(
        _mixers_kernel,
        grid=(bsz, s // tt),
        in_specs=in_specs,
        out_specs=[out, out, out],
        out_shape=[jax.ShapeDtypeStruct((bsz, s, gw), BF16)] * 3,
        scratch_shapes=_gla_scratch(tt, gw) + _gla_scratch(tt, N_HEADS * DK_B) + gdn_scratch,
        compiler_params=_cparams(("parallel", "arbitrary")),
        name="mixers",
    )(*operands)


def _mla_pre_kernel(zc_ref, m1_ref, m2_ref, gq_ref, gkv_ref, wqa_ref, wqb_ref, wk_ref, wv_ref,
                    cosq_ref, sinq_ref, cosk_ref, qt_ref, k_ref, vt_ref):
    scale = (MLA_NOPE + MLA_ROPE) ** -0.5 * LOG2E
    cq = _rms(zc_ref[0, :, 0:MLA_Q_RANK], gq_ref[...]).astype(BF16)
    ckv = _rms(zc_ref[0, :, MLA_Q_RANK:MLA_Q_RANK + MLA_KV_RANK], gkv_ref[...]).astype(BF16)
    qa = _dot(cq, wqa_ref[...])
    qb = _dot(cq, wqb_ref[...])
    kn = _dot(ckv, wk_ref[...])
    vv = _dot(ckv, wv_ref[...])
    cosq, sinq = cosq_ref[...], sinq_ref[...]
    kr = m1_ref[0] * cosk_ref[...] + m2_ref[0] * sinq
    for h in range(N_HEADS):
        sl = slice(h * LANE, (h + 1) * LANE)
        q = (qa[:, sl] * cosq + qb[:, sl] * sinq) * scale
        qt_ref[0, sl, :] = q.T.astype(BF16)
        k_ref[0, h] = (kn[:, sl] + kr).astype(BF16)
    vvt = vv.T.astype(BF16)
    for h in range(N_HEADS):
        vt_ref[0, h * V_ROWS:h * V_ROWS + HEAD_DIM, :] = vvt[h * HEAD_DIM:(h + 1) * HEAD_DIM, :]
        vt_ref[0, h * V_ROWS + HEAD_DIM:(h + 1) * V_ROWS, :] = jnp.ones((V_ROWS - HEAD_DIM, vvt.shape[1]), BF16)


def _mla_pre(zc, zm, zm2, gq, gkv, wqa, wqb, wk, wv, cosq, sinq, cosk, tm):
    bsz, s, _ = zc.shape
    tok = lambda wd: pl.BlockSpec((1, tm, wd), lambda b, t: (b, t, 0))
    tab = pl.BlockSpec((tm, LANE), lambda b, t: (t, 0))
    tr = lambda rows: pl.BlockSpec((1, rows, tm), lambda b, t: (b, 0, t))
    return pl.pallas_call(
        _mla_pre_kernel,
        grid=(bsz, s // tm),
        in_specs=[tok(zc.shape[2]), tok(LANE), tok(LANE), _row(gq), _row(gkv),
                  _row(wqa), _row(wqb), _row(wk), _row(wv), tab, tab, tab],
        out_specs=[tr(N_HEADS * LANE),
                   pl.BlockSpec((1, N_HEADS, tm, LANE), lambda b, t: (b, 0, t, 0)),
                   tr(N_HEADS * V_ROWS)],
        out_shape=[jax.ShapeDtypeStruct((bsz, N_HEADS * LANE, s), BF16),
                   jax.ShapeDtypeStruct((bsz, N_HEADS, s, LANE), BF16),
                   jax.ShapeDtypeStruct((bsz, N_HEADS * V_ROWS, s), BF16)],
        compiler_params=_cparams(("parallel", "parallel")),
        name="mla_pre",
    )(zc, zm, zm2, gq, gkv, wqa, wqb, wk, wv, cosq, sinq, cosk)


def _mla_attn_kernel(qi_ref, kj_ref, qt_ref, k_ref, vt_ref, o_ref, m_ref, acc_ref):
    pair = pl.program_id(1)
    i = qi_ref[pair]
    j = kj_ref[pair]
    tq = qt_ref.shape[2]
    tk = k_ref.shape[2]

    @pl.when(j == 0)
    def _():
        m_ref[...] = jnp.full_like(m_ref, MASK_VALUE)
        acc_ref[...] = jnp.zeros_like(acc_ref)

    nk = tk // KEY_SPLIT

    def sub_step(k0, q0, masked):
        ks = slice(k0, k0 + nk)
        qs = slice(q0, tq)
        scores = [_dot(k_ref[0, h, ks, :], qt_ref[0, h * LANE:(h + 1) * LANE, qs]) for h in range(N_HEADS)]
        if masked:
            keep = (lax.broadcasted_iota(jnp.int32, (nk, tq - q0), 0) + k0
                    <= lax.broadcasted_iota(jnp.int32, (nk, tq - q0), 1) + q0)
            scores = [jnp.where(keep, s, MASK_VALUE) for s in scores]
        m_prevs = [m_ref[h:h + 1, qs] for h in range(N_HEADS)]
        m_news = [jnp.maximum(mp, jnp.max(s, axis=0, keepdims=True)) for mp, s in zip(m_prevs, scores)]
        ps = [jnp.exp2(s - mn) for s, mn in zip(scores, m_news)]
        for h in range(N_HEADS):
            alpha = jnp.exp2(m_prevs[h] - m_news[h])
            hs = slice(h * V_ROWS, (h + 1) * V_ROWS)
            acc_ref[hs, qs] = alpha * acc_ref[hs, qs] + _dot(vt_ref[0, hs, ks], ps[h].astype(BF16))
            m_ref[h:h + 1, qs] = m_news[h]

    @pl.when(j < i)
    def _():
        for r in range(KEY_SPLIT):
            sub_step(r * nk, 0, False)

    @pl.when(j == i)
    def _():
        for r in range(KEY_SPLIT):
            sub_step(r * nk, r * nk, True)
        outs = [acc_ref[h * V_ROWS:h * V_ROWS + HEAD_DIM, :] / acc_ref[h * V_ROWS + HEAD_DIM:h * V_ROWS + HEAD_DIM + 1, :]
                for h in range(N_HEADS)]
        o_ref[0] = jnp.concatenate(outs, axis=0).T.astype(o_ref.dtype)


def _mla_attn(qt, k, vt, tq):
    bsz, _, s = qt.shape
    nq = s // tq
    pairs = [(i, j) for i in range(nq) for j in range(i + 1)]
    qi = jnp.array([p[0] for p in pairs], jnp.int32)
    kj = jnp.array([p[1] for p in pairs], jnp.int32)
    grid_spec = pltpu.PrefetchScalarGridSpec(
        num_scalar_prefetch=2,
        grid=(bsz, len(pairs)),
        in_specs=[pl.BlockSpec((1, N_HEADS * LANE, tq), lambda b, p, qi, kj: (b, 0, qi[p])),
                  pl.BlockSpec((1, N_HEADS, tq, LANE), lambda b, p, qi, kj: (b, 0, kj[p], 0)),
                  pl.BlockSpec((1, N_HEADS * V_ROWS, tq), lambda b, p, qi, kj: (b, 0, kj[p]))],
        out_specs=pl.BlockSpec((1, tq, GROUP_WIDTH), lambda b, p, qi, kj: (b, qi[p], 0)),
        scratch_shapes=[pltpu.VMEM((SUBLANE, tq), F32), pltpu.VMEM((N_HEADS * V_ROWS, tq), F32)],
    )
    return pl.pallas_call(
        _mla_attn_kernel,
        grid_spec=grid_spec,
        out_shape=jax.ShapeDtypeStruct((bsz, s, GROUP_WIDTH), BF16),
        compiler_params=_cparams(("parallel", "arbitrary")),
        name="mla_attn",
    )(qi, kj, qt, k, vt)


def _gelu_tanh(x):
    return 0.5 * x * (1.0 + jnp.tanh(0.7978845608028654 * (x + 0.044715 * (x * x * x))))


def _ffn_kernel(oa_ref, ob_ref, oc_ref, od_ref, wo_ref, g0_ref, x_ref,
                g1_ref, wg_ref, wu_ref, cw_ref, wd_ref, g2_ref, y_ref, prev_ref, act_ref, *, fb):
    tm = x_ref.shape[1]
    f = wg_ref.shape[1]

    @pl.when(pl.program_id(1) == 0)
    def _():
        prev_ref[...] = jnp.zeros_like(prev_ref)

    mix = jnp.concatenate([oa_ref[0], ob_ref[0], oc_ref[0], od_ref[0]], axis=1)
    x = x_ref[0] + _rms(_dot(mix, wo_ref[...]), g0_ref[...])
    h = _rms(x, g1_ref[...]).astype(BF16)
    row = lax.broadcasted_iota(jnp.int32, (tm, fb), 0)
    for n in range(f // fb):
        cs = slice(n * fb, (n + 1) * fb)
        gate = _dot(h, wg_ref[:, cs])
        up = _dot(h, wu_ref[:, cs])
        p1 = prev_ref[1:2, cs]
        p2 = prev_ref[0:1, cs]
        g1 = jnp.where(row == 0, p1, pltpu.roll(gate, 1, 0))
        g2 = jnp.where(row == 0, p2, jnp.where(row == 1, p1, pltpu.roll(gate, 2, 0)))
        prev_ref[0:2, cs] = gate[tm - 2:tm, :]
        conv = cw_ref[2:3, cs] * gate + cw_ref[1:2, cs] * g1 + cw_ref[0:1, cs] * g2
        act_ref[:, cs] = (_gelu_tanh(conv) * up).astype(BF16)
    y_ref[0] = x + _rms(_dot(act_ref[...], wd_ref[...]), g2_ref[...])


def _ffn(outs, wo, g0, x, g1, wg, wu, cw, wd, g2, layer, tm, fb):
    bsz, s, d = x.shape
    grp = pl.BlockSpec((1, tm, GROUP_WIDTH), lambda b, t: (b, t, 0))
    once = lambda a: pl.BlockSpec((None,) + a.shape[1:], lambda b, t: (layer, 0, 0), pipeline_mode=pl.Buffered(1))
    return pl.pallas_call(
        functools.partial(_ffn_kernel, fb=fb),
        grid=(bsz, s // tm),
        in_specs=[grp, grp, grp, grp, once(wo), _row(g0),
                  pl.BlockSpec((1, tm, d), lambda b, t: (b, t, 0)),
                  _row(g1), once(wg), once(wu), _row(cw), once(wd), _row(g2)],
        out_specs=pl.BlockSpec((1, tm, d), lambda b, t: (b, t, 0)),
        out_shape=jax.ShapeDtypeStruct((bsz, s, d), F32),
        scratch_shapes=[pltpu.VMEM((SUBLANE, wg.shape[2]), F32),
                        pltpu.VMEM((tm, wg.shape[2]), BF16)],
        compiler_params=_cparams(("parallel", "arbitrary")),
        name="ffn",
    )(*outs, wo, g0, x, g1, wg, wu, cw, wd, g2)


def _cat(pieces, like):
    lead = like.shape[:-1]
    return jnp.concatenate([jnp.zeros(lead + (p,), like.dtype) if isinstance(p, int) else p for p in pieces], axis=-1)


def _rot_cols(w):
    half = w.shape[-1] // 2
    return jnp.concatenate([-w[..., half:], w[..., :half]], axis=-1)


def _in_proj_layout(w):
    gw, dkb = GROUP_WIDTH, N_HEADS * DK_B
    o_b = 4 * gw
    o_code = o_b + 2 * dkb + gw
    o_bg = o_code + GLA_GATE_RANK
    o_c = o_bg + gw
    o_kr = o_c + MLA_Q_RANK + MLA_KV_RANK
    o_d = o_kr + MLA_ROPE
    o_beta = o_d + 3 * gw
    o_a = o_beta + N_HEADS
    o_z = o_a + N_HEADS
    assert o_z + gw == w.shape[-1]
    assert (M_CODE, M_BETA, M_A) == (0, GLA_GATE_RANK, GLA_GATE_RANK + N_HEADS)
    kr = w[..., o_kr:o_d]
    misc_gap = M_KR - (M_A + N_HEADS)
    tail_gap = LANE - M_KR - MLA_ROPE
    cols = [w[..., 0:o_b],
            w[..., o_b:o_code], w[..., o_bg:o_c],
            w[..., o_c:o_kr],
            w[..., o_d:o_beta], w[..., o_z:],
            w[..., o_code:o_bg], w[..., o_beta:o_a], w[..., o_a:o_z], misc_gap, kr, tail_gap,
            M_KR, _rot_cols(kr), tail_gap]
    widths = (4 * gw, 2 * dkb + 2 * gw, MLA_Q_RANK + MLA_KV_RANK, 4 * gw, LANE, LANE)
    return _cat(cols, w).astype(BF16), widths


def _mla_layout(w_uq, w_ukv):
    dq = MLA_NOPE + MLA_ROPE
    pad = LANE - dq
    qa, qb, wk, wv = [], [], [], []
    for h in range(N_HEADS):
        nope = w_uq[..., h * dq:h * dq + MLA_NOPE]
        rope = w_uq[..., h * dq + MLA_NOPE:(h + 1) * dq]
        qa += [nope, rope, pad]
        qb += [MLA_NOPE, _rot_cols(rope), pad]
        wk += [w_ukv[..., h * 2 * HEAD_DIM:h * 2 * HEAD_DIM + MLA_NOPE], LANE - MLA_NOPE]
        wv += [w_ukv[..., h * 2 * HEAD_DIM + MLA_NOPE:(h + 1) * 2 * HEAD_DIM]]
    return tuple(_cat(p, like).astype(BF16) for p, like in ((qa, w_uq), (qb, w_uq), (wk, w_ukv), (wv, w_ukv)))


def _rope_tables(s):
    inv = ROPE_THETA ** (-jnp.arange(0, MLA_ROPE, 2, dtype=F32) / MLA_ROPE)
    ang = jnp.arange(s, dtype=F32)[:, None] * inv[None, :]
    cos, sin = jnp.cos(ang), jnp.sin(ang)
    pad = LANE - MLA_NOPE - MLA_ROPE
    cosq = _cat([jnp.ones((s, MLA_NOPE), F32), cos, cos, pad], cos)
    sinq = _cat([MLA_NOPE, sin, sin, pad], cos)
    cosk = _cat([MLA_NOPE, cos, cos, pad], cos)
    return cosq, sinq, cosk


def _tile_heads(g):
    return jnp.tile(g, N_HEADS)[None, :]


def _pick(n, prefs):
    for p in prefs:
        if n % p == 0:
            return p
    return n


def kernel(x, w_in, w_out, pre_mix_g, post_mix_g, pre_ffn_g, post_ffn_g, hgrn_lb_logits, hgrn_norm_g,
           gla_w_gk2, gla_b_gk, gla_norm_g, mla_q_norm_g, mla_w_uq, mla_kv_norm_g, mla_w_ukv,
           gdn_conv_w, gdn_a_log, gdn_dt_bias, gdn_norm_g, ffn_w_gate, ffn_w_up, ffn_conv_w, ffn_w_down):
    bsz, s, d = x.shape
    depth = w_in.shape[0]
    t = bsz * s
    tm = _pick(t, (512, 256))
    tt = _pick(s, (256,))
    ts = _pick(s, (512, 256))
    tq = _pick(s, (1024, 512, 256, 128))
    fb = _pick(ffn_w_gate.shape[2], (512, 256))

    cosq, sinq, cosk = _rope_tables(s)
    p = jax.nn.softmax(hgrn_lb_logits.astype(F32), axis=0)
    lower_bounds = jnp.cumsum(p, axis=0) - p[0]

    lanes = jnp.arange(LANE)
    head_of = jnp.arange(GROUP_WIDTH) // HEAD_DIM
    ebeta = (lanes[:, None] == M_BETA + head_of[None, :]).astype(BF16)
    ea = (lanes[:, None] == M_A + head_of[None, :]).astype(BF16)

    w_in_all, widths = _in_proj_layout(w_in)
    wqa, wqb, wk, wv = _mla_layout(mla_w_uq, mla_w_ukv)
    wgk = jnp.concatenate([gla_w_gk2, jnp.zeros((depth, LANE - GLA_GATE_RANK, N_HEADS * DK_B), F32)],
                          axis=1).astype(BF16)
    a_tail = LANE - M_A - N_HEADS
    apad = _cat([M_A, jnp.exp(gdn_a_log.astype(F32)), a_tail], gdn_a_log)
    dtpad = _cat([M_A, gdn_dt_bias, a_tail], gdn_dt_bias)
    w_out16, wg16, wu16, wd16 = (a.astype(BF16) for a in (w_out, ffn_w_gate, ffn_w_up, ffn_w_down))

    for l in range(depth):
        za, zb, zc, zd, zm, zm2 = _in_proj(x.reshape(t, d), pre_mix_g[l][None, :], w_in_all, l, widths, tm)
        r3 = lambda a: a.reshape(bsz, s, a.shape[-1])
        za, zb, zc, zd, zm, zm2 = map(r3, (za, zb, zc, zd, zm, zm2))

        qt, kh, vt = _mla_pre(zc, zm, zm2, mla_q_norm_g[l][None, :], mla_kv_norm_g[l][None, :],
                              wqa[l], wqb[l], wk[l], wv[l], cosq, sinq, cosk, tq)
        o_c = _mla_attn(qt, kh, vt, tq)
        o_a, o_b, o_d = _mixers(
            za, zb, zd, zm,
            (lower_bounds[l][None, :], _tile_heads(hgrn_norm_g[l])),
            (wgk[l], gla_b_gk[l][None, :], _tile_heads(gla_norm_g[l])),
            (gdn_conv_w[l], apad[l][None, :], dtpad[l][None, :], ebeta, ea, _tile_heads(gdn_norm_g[l])), tt)

        x = _ffn((o_a, o_b, o_c, o_d), w_out16, post_mix_g[l][None, :], x,
                 pre_ffn_g[l][None, :], wg16, wu16, ffn_conv_w[l], wd16, post_ffn_g[l][None, :],
                 layer=l, tm=ts, fb=fb)
    return x
```

```python
import functools

import jax
import jax.numpy as jnp
from jax import lax
from jax.experimental import pallas as pl
from jax.experimental.pallas import tpu as pltpu

F32 = jnp.float32
BF16 = jnp.bfloat16
HIGHEST = lax.Precision.HIGHEST

N_GROUPS = 4
HEAD_DIM = 64
N_HEADS = 4
GROUP_WIDTH = N_HEADS * HEAD_DIM
EPS = 1e-6
MASK_VALUE = -1e30
MIN_POS = 1e-30
LOG2E = 1.4426950408889634

DK_B = HEAD_DIM // 2
GLA_GATE_RANK = 16
GLA_GATE_NORMALIZER = 16.0
MLA_Q_RANK = 256
MLA_KV_RANK = 128
MLA_NOPE = 64
MLA_ROPE = 32
ROPE_THETA = 10000.0
GDN_CONV = 4
FFN_CONV = 3
LANE = 128
SUBLANE = 8

M_CODE = 0
M_BETA = 16
M_A = 20
M_KR = 64

BAND = 16
GDN_CHUNK = 64
KEY_SPLIT = 8
V_ROWS = HEAD_DIM + 16
VMEM_LIMIT = 56 * 1024 * 1024


def _cparams(sem):
    return pltpu.CompilerParams(dimension_semantics=sem, vmem_limit_bytes=VMEM_LIMIT)


def _rms(x, g):
    return x * lax.rsqrt(jnp.mean(x * x, axis=-1, keepdims=True) + EPS) * g


def _sigmoid(x):
    return jax.nn.sigmoid(x)


def _softplus(x):
    return jnp.maximum(x, 0.0) + jnp.log1p(jnp.exp(-jnp.abs(x)))


def _block_cumsum(x, rowmod, block):
    s = 1
    while s < block:
        x = x + jnp.where(rowmod >= s, pltpu.roll(x, s, 0), 0.0)
        s *= 2
    return x


def _dot(a, b, dims=(((1,), (0,)), ((), ())), precision=None):
    return lax.dot_general(a, b, dims, precision=precision, preferred_element_type=F32)


NT = (((1,), (1,)), ((), ()))
TN = (((0,), (0,)), ((), ()))


def _in_proj_kernel(x_ref, g_ref, w_ref, *out_refs):
    h = _rms(x_ref[...], g_ref[...]).astype(BF16)
    z = _dot(h, w_ref[...])
    off = 0
    for o_ref in out_refs:
        n = o_ref.shape[-1]
        o_ref[...] = z[:, off:off + n]
        off += n


def _in_proj(x2, g, w_all, layer, widths, tm):
    t, d = x2.shape
    n = w_all.shape[2]
    return pl.pallas_call(
        _in_proj_kernel,
        grid=(t // tm,),
        in_specs=[pl.BlockSpec((tm, d), lambda i: (i, 0)),
                  pl.BlockSpec((1, d), lambda i: (0, 0)),
                  pl.BlockSpec((None, d, n), lambda i: (layer, 0, 0))],
        out_specs=[pl.BlockSpec((tm, wd), lambda i: (i, 0)) for wd in widths],
        out_shape=[jax.ShapeDtypeStruct((t, wd), F32) for wd in widths],
        compiler_params=_cparams(("parallel",)),
        name="in_proj",
    )(x2, g, w_all)


def _gla_core(q, k, v, lg, seg, same_blk, s_ref, oacc_ref, kv_ref, sc_ref, dkh):
    tt, dk = q.shape
    dv = v.shape[1]
    half = BAND // 2
    nblk = tt // BAND
    rowmod = lax.broadcasted_iota(jnp.int32, (tt, dk), 0) & (BAND - 1)
    b = _block_cumsum(lg * LOG2E, rowmod, BAND)
    upper = rowmod >= half
    mid = jnp.broadcast_to(b.reshape(nblk, BAND, dk)[:, half - 1:half, :], (nblk, BAND, dk)).reshape(tt, dk)
    bh = jnp.where(upper, b - mid, b)
    rowmod_h = rowmod & (half - 1)
    yield

    def roll_half(x, d):
        return pltpu.roll(x.reshape(tt // half, half, x.shape[1]), d, 1).reshape(tt, x.shape[1])

    def block_kv(n):
        r0 = n * BAND
        bn = b[r0:r0 + BAND]
        kt = (k[r0:r0 + BAND] * jnp.exp2(bn[BAND - 1:BAND] - bn)).astype(BF16)
        kv_ref[n] = _dot(v[r0:r0 + BAND].astype(BF16), kt, TN)

    qe = (q * jnp.exp2(b)).astype(BF16)
    head_rows = [slice(h * HEAD_DIM, (h + 1) * HEAD_DIM) for h in range(N_HEADS)]
    head_lanes = [slice((h * dkh // LANE) * LANE, (h * dkh // LANE) * LANE + LANE) for h in range(N_HEADS)]
    own = [(lax.broadcasted_iota(jnp.int32, (HEAD_DIM, LANE), 1) + lt.start) // dkh == h
           for h, lt in enumerate(head_lanes)]
    sts = [s_ref[hs, lt] for hs, lt in zip(head_rows, head_lanes)]

    def advance(n):
        r0 = n * BAND
        for h, (hs, lt) in enumerate(zip(head_rows, head_lanes)):
            sc_ref[n, hs, lt] = jnp.where(own[h], sts[h], 0.0).astype(BF16)
            sts[h] = sts[h] * jnp.exp2(b[r0 + BAND - 1:r0 + BAND, lt]) + kv_ref[n, hs, lt]
        oacc_ref[r0:r0 + BAND, :] = _dot(qe[r0:r0 + BAND], sc_ref[n], NT)

    q_up = (q * jnp.exp2(jnp.where(upper, bh, MASK_VALUE))).astype(BF16)
    k_lo = (k * jnp.exp2(jnp.where(upper, MASK_VALUE, mid - b))).astype(BF16)
    v16 = v.astype(BF16)
    cross = [_dot(q_up[:, h * dkh:(h + 1) * dkh], k_lo[:, h * dkh:(h + 1) * dkh], NT) * same_blk
             for h in range(N_HEADS)]
    cross_out = [_dot(a.astype(BF16), v16[:, h * HEAD_DIM:(h + 1) * HEAD_DIM]) for h, a in enumerate(cross)]
    yield

    per_step = -(-nblk // half)
    o = _dot((q * k).astype(BF16), seg) * v
    for n in range(per_step):
        block_kv(n)
    for d in range(1, half):
        kd = roll_half(k, d)
        bd = roll_half(bh, d)
        vd = roll_half(v, d)
        p = q * kd * jnp.exp2(jnp.where(rowmod_h >= d, bh - bd, MASK_VALUE))
        o = o + _dot(p.astype(BF16), seg) * vd
        for n in range(d * per_step, min((d + 1) * per_step, nblk)):
            block_kv(n)
        for n in range((d - 1) * per_step, min(d * per_step, nblk)):
            advance(n)
        yield
    for n in range((half - 1) * per_step, nblk):
        advance(n)
    for h, (hs, lt) in enumerate(zip(head_rows, head_lanes)):
        s_ref[hs, lt] = sts[h]

    for h, c_out in enumerate(cross_out):
        hv = slice(h * HEAD_DIM, (h + 1) * HEAD_DIM)
        oacc_ref[:, hv] = oacc_ref[:, hv] + c_out
    return o + oacc_ref[...]


def _head_sum(x, seg):
    hi = x.astype(BF16)
    lo = (x - hi.astype(F32)).astype(BF16)
    return _dot(hi, seg) + _dot(lo, seg)


def _head_norm_gate(o, ng, gate_act, seg, o_ref):
    ms = _head_sum(o * o, seg) * (1.0 / HEAD_DIM)
    o_ref[0] = (o * lax.rsqrt(ms + EPS) * ng * gate_act).astype(o_ref.dtype)


def _hgrn_body(z_ref, lb_ref, ng_ref, seg_ref, blk_ref, o_ref, s_ref, oacc_ref, kv_ref, sc_ref):
    gw = GROUP_WIDTH
    q = z_ref[0, :, 0:gw]
    zf = z_ref[0, :, gw:2 * gw]
    v = z_ref[0, :, 2 * gw:3 * gw]
    gate = z_ref[0, :, 3 * gw:4 * gw]
    lb = lb_ref[...]
    f = lb + (1.0 - lb) * _sigmoid(zf)
    lg = jnp.log(jnp.maximum(f, MIN_POS))
    k = (1.0 - lb) * _sigmoid(-zf)
    qh = q * _sigmoid(q) * HEAD_DIM ** -0.5
    yield
    o = yield from _gla_core(qh, k, v, lg, seg_ref[...], blk_ref[...], s_ref, oacc_ref, kv_ref, sc_ref, HEAD_DIM)
    yield
    _head_norm_gate(o, ng_ref[...], _sigmoid(gate), seg_ref[...], o_ref)


def _gla_body(z_ref, m_ref, wgk_ref, bgk_ref, ng_ref, segk_ref, seg_ref, blk_ref, o_ref, s_ref, oacc_ref, kv_ref, sc_ref):
    dk = N_HEADS * DK_B
    q = z_ref[0, :, 0:dk]
    k = z_ref[0, :, dk:2 * dk]
    v = z_ref[0, :, 2 * dk:2 * dk + GROUP_WIDTH]
    gate = z_ref[0, :, 2 * dk + GROUP_WIDTH:2 * dk + 2 * GROUP_WIDTH]
    x = _dot(m_ref[0].astype(BF16), wgk_ref[...]) + bgk_ref[...]
    lg = -_softplus(-x) / GLA_GATE_NORMALIZER
    yield
    o = yield from _gla_core(q * DK_B ** -0.5, k, v, lg, segk_ref[...], blk_ref[...], s_ref, oacc_ref, kv_ref, sc_ref,
                             DK_B)
    yield
    _head_norm_gate(o, ng_ref[...], gate * _sigmoid(gate), seg_ref[...], o_ref)


def _row(a):
    return pl.BlockSpec(a.shape, lambda b, t: (0,) * a.ndim)


def _gla_scratch(tt, dk):
    nblk = tt // BAND
    return [pltpu.VMEM((GROUP_WIDTH, dk), F32),
            pltpu.VMEM((tt, GROUP_WIDTH), F32),
            pltpu.VMEM((nblk, GROUP_WIDTH, dk), F32),
            pltpu.VMEM((nblk, GROUP_WIDTH, dk), BF16)]


def _gdn_body(z_ref, m_ref, cw_ref, apad_ref, dtpad_ref, ebeta_ref, ea_ref, ng_ref, seg_ref, headf_ref,
              incl_ref, nstrict_ref, o_ref, xs_ref, s_ref, oacc_ref, u_ref, w_ref, aqk_ref, sc_ref):
    tt = z_ref.shape[1]
    gw = GROUP_WIDTH
    c = GDN_CHUNK
    halo = SUBLANE

    xs_ref[halo:halo + tt, :] = z_ref[0, :, 0:3 * gw]
    acc = cw_ref[GDN_CONV - 1:GDN_CONV, :] * xs_ref[halo:halo + tt, :]
    for j in range(1, GDN_CONV):
        acc = acc + cw_ref[GDN_CONV - 1 - j:GDN_CONV - j, :] * xs_ref[halo - j:halo - j + tt, :]
    xs_ref[0:halo, :] = xs_ref[tt:tt + halo, :]
    qkv = acc * _sigmoid(acc)
    yield

    q_all, k_all, v_all = qkv[:, 0:gw], qkv[:, gw:2 * gw], qkv[:, 2 * gw:3 * gw]

    seg = seg_ref[...]
    head_f = headf_ref[...]
    def expand(x, e_ref, terms):
        acc, r = None, x
        for _ in range(terms):
            hi = r.astype(BF16)
            r = r - hi.astype(F32)
            part = _dot(hi, e_ref[...])
            acc = part if acc is None else acc + part
        return acc

    qn = q_all * lax.rsqrt(_head_sum(q_all * q_all, seg) + EPS) * HEAD_DIM ** -0.5
    kn = k_all * lax.rsqrt(_head_sum(k_all * k_all, seg) + EPS)
    yield

    m = m_ref[0]
    beta_x = expand(_sigmoid(m), ebeta_ref, 2)
    lg_m = -apad_ref[...] * _softplus(m + dtpad_ref[...])
    rowmod_m = lax.broadcasted_iota(jnp.int32, (tt, LANE), 0) & (c - 1)
    b_m = _block_cumsum(lg_m, rowmod_m, c)
    b_x = expand(b_m, ea_ref, 3)
    eb = jnp.exp(b_x)
    kb = kn * beta_x
    vb = v_all * beta_x
    kbe = kb * eb
    qe = qn * eb
    yield

    incl = incl_ref[...]
    nstrict = nstrict_ref[...]

    heads = [slice(h * HEAD_DIM, (h + 1) * HEAD_DIM) for h in range(N_HEADS)]
    nms = []
    for h, sl in enumerate(heads):
        bmat = jnp.broadcast_to(b_m[:, M_A + h:M_A + h + 1], (tt, tt))
        decay = jnp.exp(jnp.minimum(bmat - bmat.T, 0.0))
        kh16 = kn[:, sl].astype(BF16)
        nms.append(_dot(kb[:, sl].astype(BF16), kh16, NT) * (decay * nstrict))
        aqk_ref[h] = (_dot(qn[:, sl].astype(BF16), kh16, NT) * (decay * incl)).astype(BF16)
        if h % 2 == 1:
            yield
    tns = list(nms)
    pws16 = [p.astype(BF16) for p in nms]
    for _ in range(5):
        pws = [_dot(p, p) for p in pws16]
        pws16 = [p.astype(BF16) for p in pws]
        tns = [t + p + _dot(t.astype(BF16), p16) for t, p, p16 in zip(tns, pws, pws16)]
        yield
    for sl, tn in zip(heads, tns):
        tn16 = tn.astype(BF16)
        u_ref[:, sl] = vb[:, sl] + _dot(tn16, vb[:, sl].astype(BF16))
        w_ref[:, sl] = kbe[:, sl] + _dot(tn16, kbe[:, sl].astype(BF16))
    yield

    decays, pms, rms = [], [], []
    for n in range(tt // c):
        rs = slice(n * c, (n + 1) * c)
        b_last = b_x[n * c + c - 1:(n + 1) * c, :]
        kt_t = (kn[rs] * jnp.exp(b_last - b_x[rs])).T.astype(BF16)
        decays.append(jnp.exp(b_last))
        pms.append((_dot(kt_t, w_ref[rs, :].astype(BF16)) * head_f).astype(BF16))
        rms.append(_dot(kt_t, u_ref[rs, :].astype(BF16)) * head_f)
    yield
    st = s_ref[...]
    for n in range(tt // c):
        st16 = st.astype(BF16)
        sc_ref[n] = st16
        st = st * decays[n] - _dot(pms[n], st16) + rms[n]
    s_ref[...] = st
    yield

    for n in range(tt // c):
        rs = slice(n * c, (n + 1) * c)
        lhs = jnp.concatenate([w_ref[rs, :], qe[rs]], axis=0).astype(BF16)
        res = _dot(lhs, sc_ref[n])
        u_ref[rs, :] = u_ref[rs, :] - res[0:c]
        oacc_ref[rs, :] = res[c:2 * c]
    for h in range(N_HEADS):
        sl = slice(h * HEAD_DIM, (h + 1) * HEAD_DIM)
        oacc_ref[:, sl] = oacc_ref[:, sl] + _dot(aqk_ref[h], u_ref[:, sl].astype(BF16))
    yield

    zg = z_ref[0, :, 3 * gw:4 * gw]
    _head_norm_gate(oacc_ref[...], ng_ref[...], zg * _sigmoid(zg), seg, o_ref)


def _interleave(*stages):
    live = list(stages)
    while live:
        for g in list(live):
            try:
                next(g)
            except StopIteration:
                live.remove(g)


N_HGRN_IN, N_GLA_IN, N_GDN_IN = 5, 8, 12
N_GLA_SCRATCH, N_GDN_SCRATCH = 4, 7
GDN_LEAD_STAGES = 6


def _mixers_kernel(*refs):
    it = iter(refs)
    take = lambda n: [next(it) for _ in range(n)]
    hgrn_in, gla_in, gdn_in = take(N_HGRN_IN), take(N_GLA_IN), take(N_GDN_IN)
    oa_ref, ob_ref, od_ref = take(3)
    hgrn_scr, gla_scr, gdn_scr = take(N_GLA_SCRATCH), take(N_GLA_SCRATCH), take(N_GDN_SCRATCH)

    @pl.when(pl.program_id(1) == 0)
    def _():
        for scr in (hgrn_scr, gla_scr):
            scr[0][...] = jnp.zeros_like(scr[0])
            scr[3][...] = jnp.zeros_like(scr[3])
        gdn_scr[1][...] = jnp.zeros_like(gdn_scr[1])
        gdn_scr[0][0:SUBLANE, :] = jnp.zeros((SUBLANE, gdn_scr[0].shape[1]), F32)

    gdn = _gdn_body(*gdn_in, od_ref, *gdn_scr)
    for _ in range(GDN_LEAD_STAGES):
        next(gdn)
    _interleave(gdn,
                _hgrn_body(*hgrn_in, oa_ref, *hgrn_scr),
                _gla_body(*gla_in, ob_ref, *gla_scr))


def _mixers(za, zb, zd, zm, hgrn_params, gla_params, gdn_params, tt):
    bsz, s, _ = za.shape
    gw = GROUP_WIDTH
    tok = lambda a: pl.BlockSpec((1, tt, a.shape[2]), lambda b, t: (b, t, 0))
    rows = jnp.arange(tt)
    same_band = (rows[:, None] // BAND == rows[None, :] // BAND).astype(F32)
    same_chunk = rows[:, None] // GDN_CHUNK == rows[None, :] // GDN_CHUNK
    incl = (same_chunk & (rows[:, None] >= rows[None, :])).astype(F32)
    nstrict = -(same_chunk & (rows[:, None] > rows[None, :])).astype(F32)
    lane_head = jnp.arange(gw) // HEAD_DIM
    head_f = (lane_head[:, None] == lane_head[None, :]).astype(F32)
    seg = head_f.astype(BF16)
    seg_k = (jnp.arange(N_HEADS * DK_B)[:, None] // DK_B == lane_head[None, :]).astype(BF16)
    hgrn_params = (*hgrn_params, seg, same_band)
    gla_params = (*gla_params, seg_k, seg, same_band)
    gdn_params = (*gdn_params, seg, head_f, incl, nstrict)
    operands = [za, *hgrn_params, zb, zm, *gla_params, zd, zm, *gdn_params]
    in_specs = ([tok(za)] + [_row(a) for a in hgrn_params]
                + [tok(zb), tok(zm)] + [_row(a) for a in gla_params]
                + [tok(zd), tok(zm)] + [_row(a) for a in gdn_params])
    assert (len(hgrn_params) + 1, len(gla_params) + 2, len(gdn_params) + 2) == (N_HGRN_IN, N_GLA_IN, N_GDN_IN)
    out = pl.BlockSpec((1, tt, gw), lambda b, t: (b, t, 0))
    gdn_scratch = [pltpu.VMEM((tt + 2 * SUBLANE, 3 * gw), F32),
                   pltpu.VMEM((gw, gw), F32),
                   pltpu.VMEM((tt, gw), F32),
                   pltpu.VMEM((tt, gw), F32),
                   pltpu.VMEM((tt, gw), F32),
                   pltpu.VMEM((N_HEADS, tt, tt), BF16),
                   pltpu.VMEM((tt // GDN_CHUNK, gw, gw), BF16)]
    assert len(gdn_scratch) == N_GDN_SCRATCH and len(_gla_scratch(tt, gw)) == N_GLA_SCRATCH
    return pl.pallas_call(
        _mixers_kernel,
        grid=(bsz, s // tt),
        in_specs=in_specs,
        out_specs=[out, out, out],
        out_shape=[jax.ShapeDtypeStruct((bsz, s, gw), BF16)] * 3,
        scratch_shapes=_gla_scratch(tt, gw) + _gla_scratch(tt, N_HEADS * DK_B) + gdn_scratch,
        compiler_params=_cparams(("parallel", "arbitrary")),
        name="mixers",
    )(*operands)


def _mla_pre_kernel(zc_ref, m1_ref, m2_ref, gq_ref, gkv_ref, wqa_ref, wqb_ref, wk_ref, wv_ref,
                    cosq_ref, sinq_ref, cosk_ref, qt_ref, k_ref, vt_ref):
    scale = (MLA_NOPE + MLA_ROPE) ** -0.5 * LOG2E
    cq = _rms(zc_ref[0, :, 0:MLA_Q_RANK], gq_ref[...]).astype(BF16)
    ckv = _rms(zc_ref[0, :, MLA_Q_RANK:MLA_Q_RANK + MLA_KV_RANK], gkv_ref[...]).astype(BF16)
    qa = _dot(cq, wqa_ref[...])
    qb = _dot(cq, wqb_ref[...])
    kn = _dot(ckv, wk_ref[...])
    vv = _dot(ckv, wv_ref[...])
    cosq, sinq = cosq_ref[...], sinq_ref[...]
    kr = m1_ref[0] * cosk_ref[...] + m2_ref[0] * sinq
    for h in range(N_HEADS):
        sl = slice(h * LANE, (h + 1) * LANE)
        q = (qa[:, sl] * cosq + qb[:, sl] * sinq) * scale
        qt_ref[0, sl, :] = q.T.astype(BF16)
        k_ref[0, h] = (kn[:, sl] + kr).astype(BF16)
    vvt = vv.T.astype(BF16)
    for h in range(N_HEADS):
        vt_ref[0, h * V_ROWS:h * V_ROWS + HEAD_DIM, :] = vvt[h * HEAD_DIM:(h + 1) * HEAD_DIM, :]
        vt_ref[0, h * V_ROWS + HEAD_DIM:(h + 1) * V_ROWS, :] = jnp.ones((V_ROWS - HEAD_DIM, vvt.shape[1]), BF16)


def _mla_pre(zc, zm, zm2, gq, gkv, wqa, wqb, wk, wv, cosq, sinq, cosk, tm):
    bsz, s, _ = zc.shape
    tok = lambda wd: pl.BlockSpec((1, tm, wd), lambda b, t: (b, t, 0))
    tab = pl.BlockSpec((tm, LANE), lambda b, t: (t, 0))
    tr = lambda rows: pl.BlockSpec((1, rows, tm), lambda b, t: (b, 0, t))
    return pl.pallas_call(
        _mla_pre_kernel,
        grid=(bsz, s // tm),
        in_specs=[tok(zc.shape[2]), tok(LANE), tok(LANE), _row(gq), _row(gkv),
                  _row(wqa), _row(wqb), _row(wk), _row(wv), tab, tab, tab],
        out_specs=[tr(N_HEADS * LANE),
                   pl.BlockSpec((1, N_HEADS, tm, LANE), lambda b, t: (b, 0, t, 0)),
                   tr(N_HEADS * V_ROWS)],
        out_shape=[jax.ShapeDtypeStruct((bsz, N_HEADS * LANE, s), BF16),
                   jax.ShapeDtypeStruct((bsz, N_HEADS, s, LANE), BF16),
                   jax.ShapeDtypeStruct((bsz, N_HEADS * V_ROWS, s), BF16)],
        compiler_params=_cparams(("parallel", "parallel")),
        name="mla_pre",
    )(zc, zm, zm2, gq, gkv, wqa, wqb, wk, wv, cosq, sinq, cosk)


def _mla_attn_kernel(qi_ref, kj_ref, qt_ref, k_ref, vt_ref, o_ref, m_ref, acc_ref):
    pair = pl.program_id(1)
    i = qi_ref[pair]
    j = kj_ref[pair]
    tq = qt_ref.shape[2]
    tk = k_ref.shape[2]

    @pl.when(j == 0)
    def _():
        m_ref[...] = jnp.full_like(m_ref, MASK_VALUE)
        acc_ref[...] = jnp.zeros_like(acc_ref)

    nk = tk // KEY_SPLIT

    def sub_step(k0, q0, masked):
        ks = slice(k0, k0 + nk)
        qs = slice(q0, tq)
        scores = [_dot(k_ref[0, h, ks, :], qt_ref[0, h * LANE:(h + 1) * LANE, qs]) for h in range(N_HEADS)]
        if masked:
            keep = (lax.broadcasted_iota(jnp.int32, (nk, tq - q0), 0) + k0
                    <= lax.broadcasted_iota(jnp.int32, (nk, tq - q0), 1) + q0)
            scores = [jnp.where(keep, s, MASK_VALUE) for s in scores]
        m_prevs = [m_ref[h:h + 1, qs] for h in range(N_HEADS)]
        m_news = [jnp.maximum(mp, jnp.max(s, axis=0, keepdims=True)) for mp, s in zip(m_prevs, scores)]
        ps = [jnp.exp2(s - mn) for s, mn in zip(scores, m_news)]
        for h in range(N_HEADS):
            alpha = jnp.exp2(m_prevs[h] - m_news[h])
            hs = slice(h * V_ROWS, (h + 1) * V_ROWS)
            acc_ref[hs, qs] = alpha * acc_ref[hs, qs] + _dot(vt_ref[0, hs, ks], ps[h].astype(BF16))
            m_ref[h:h + 1, qs] = m_news[h]

    @pl.when(j < i)
    def _():
        for r in range(KEY_SPLIT):
            sub_step(r * nk, 0, False)

    @pl.when(j == i)
    def _():
        for r in range(KEY_SPLIT):
            sub_step(r * nk, r * nk, True)
        outs = [acc_ref[h * V_ROWS:h * V_ROWS + HEAD_DIM, :] / acc_ref[h * V_ROWS + HEAD_DIM:h * V_ROWS + HEAD_DIM + 1, :]
                for h in range(N_HEADS)]
        o_ref[0] = jnp.concatenate(outs, axis=0).T.astype(o_ref.dtype)


def _mla_attn(qt, k, vt, tq):
    bsz, _, s = qt.shape
    nq = s // tq
    pairs = [(i, j) for i in range(nq) for j in range(i + 1)]
    qi = jnp.array([p[0] for p in pairs], jnp.int32)
    kj = jnp.array([p[1] for p in pairs], jnp.int32)
    grid_spec = pltpu.PrefetchScalarGridSpec(
        num_scalar_prefetch=2,
        grid=(bsz, len(pairs)),
        in_specs=[pl.BlockSpec((1, N_HEADS * LANE, tq), lambda b, p, qi, kj: (b, 0, qi[p])),
                  pl.BlockSpec((1, N_HEADS, tq, LANE), lambda b, p, qi, kj: (b, 0, kj[p], 0)),
                  pl.BlockSpec((1, N_HEADS * V_ROWS, tq), lambda b, p, qi, kj: (b, 0, kj[p]))],
        out_specs=pl.BlockSpec((1, tq, GROUP_WIDTH), lambda b, p, qi, kj: (b, qi[p], 0)),
        scratch_shapes=[pltpu.VMEM((SUBLANE, tq), F32), pltpu.VMEM((N_HEADS * V_ROWS, tq), F32)],
    )
    return pl.pallas_call(
        _mla_attn_kernel,
        grid_spec=grid_spec,
        out_shape=jax.ShapeDtypeStruct((bsz, s, GROUP_WIDTH), BF16),
        compiler_params=_cparams(("parallel", "arbitrary")),
        name="mla_attn",
    )(qi, kj, qt, k, vt)


def _gelu_tanh(x):
    return 0.5 * x * (1.0 + jnp.tanh(0.7978845608028654 * (x + 0.044715 * (x * x * x))))


def _ffn_kernel(oa_ref, ob_ref, oc_ref, od_ref, wo_ref, g0_ref, x_ref,
                g1_ref, wg_ref, wu_ref, cw_ref, wd_ref, g2_ref, y_ref, prev_ref, act_ref, *, fb):
    tm = x_ref.shape[1]
    f = wg_ref.shape[1]

    @pl.when(pl.program_id(1) == 0)
    def _():
        prev_ref[...] = jnp.zeros_like(prev_ref)

    mix = jnp.concatenate([oa_ref[0], ob_ref[0], oc_ref[0], od_ref[0]], axis=1)
    x = x_ref[0] + _rms(_dot(mix, wo_ref[...]), g0_ref[...])
    h = _rms(x, g1_ref[...]).astype(BF16)
    row = lax.broadcasted_iota(jnp.int32, (tm, fb), 0)
    for n in range(f // fb):
        cs = slice(n * fb, (n + 1) * fb)
        gate = _dot(h, wg_ref[:, cs])
        up = _dot(h, wu_ref[:, cs])
        p1 = prev_ref[1:2, cs]
        p2 = prev_ref[0:1, cs]
        g1 = jnp.where(row == 0, p1, pltpu.roll(gate, 1, 0))
        g2 = jnp.where(row == 0, p2, jnp.where(row == 1, p1, pltpu.roll(gate, 2, 0)))
        prev_ref[0:2, cs] = gate[tm - 2:tm, :]
        conv = cw_ref[2:3, cs] * gate + cw_ref[1:2, cs] * g1 + cw_ref[0:1, cs] * g2
        act_ref[:, cs] = (_gelu_tanh(conv) * up).astype(BF16)
    y_ref[0] = x + _rms(_dot(act_ref[...], wd_ref[...]), g2_ref[...])


def _ffn(outs, wo, g0, x, g1, wg, wu, cw, wd, g2, layer, tm, fb):
    bsz, s, d = x.shape
    grp = pl.BlockSpec((1, tm, GROUP_WIDTH), lambda b, t: (b, t, 0))
    once = lambda a: pl.BlockSpec((None,) + a.shape[1:], lambda b, t: (layer, 0, 0), pipeline_mode=pl.Buffered(1))
    return pl.pallas_call(
        functools.partial(_ffn_kernel, fb=fb),
        grid=(bsz, s // tm),
        in_specs=[grp, grp, grp, grp, once(wo), _row(g0),
                  pl.BlockSpec((1, tm, d), lambda b, t: (b, t, 0)),
                  _row(g1), once(wg), once(wu), _row(cw), once(wd), _row(g2)],
        out_specs=pl.BlockSpec((1, tm, d), lambda b, t: (b, t, 0)),
        out_shape=jax.ShapeDtypeStruct((bsz, s, d), F32),
        scratch_shapes=[pltpu.VMEM((SUBLANE, wg.shape[2]), F32),
                        pltpu.VMEM((tm, wg.shape[2]), BF16)],
        compiler_params=_cparams(("parallel", "arbitrary")),
        name="ffn",
    )(*outs, wo, g0, x, g1, wg, wu, cw, wd, g2)


def _cat(pieces, like):
    lead = like.shape[:-1]
    return jnp.concatenate([jnp.zeros(lead + (p,), like.dtype) if isinstance(p, int) else p for p in pieces], axis=-1)


def _rot_cols(w):
    half = w.shape[-1] // 2
    return jnp.concatenate([-w[..., half:], w[..., :half]], axis=-1)


def _in_proj_layout(w):
    gw, dkb = GROUP_WIDTH, N_HEADS * DK_B
    o_b = 4 * gw
    o_code = o_b + 2 * dkb + gw
    o_bg = o_code + GLA_GATE_RANK
    o_c = o_bg + gw
    o_kr = o_c + MLA_Q_RANK + MLA_KV_RANK
    o_d = o_kr + MLA_ROPE
    o_beta = o_d + 3 * gw
    o_a = o_beta + N_HEADS
    o_z = o_a + N_HEADS
    assert o_z + gw == w.shape[-1]
    assert (M_CODE, M_BETA, M_A) == (0, GLA_GATE_RANK, GLA_GATE_RANK + N_HEADS)
    kr = w[..., o_kr:o_d]
    misc_gap = M_KR - (M_A + N_HEADS)
    tail_gap = LANE - M_KR - MLA_ROPE
    cols = [w[..., 0:o_b],
            w[..., o_b:o_code], w[..., o_bg:o_c],
            w[..., o_c:o_kr],
            w[..., o_d:o_beta], w[..., o_z:],
            w[..., o_code:o_bg], w[..., o_beta:o_a], w[..., o_a:o_z], misc_gap, kr, tail_gap,
            M_KR, _rot_cols(kr), tail_gap]
    widths = (4 * gw, 2 * dkb + 2 * gw, MLA_Q_RANK + MLA_KV_RANK, 4 * gw, LANE, LANE)
    return _cat(cols, w).astype(BF16), widths


def _mla_layout(w_uq, w_ukv):
    dq = MLA_NOPE + MLA_ROPE
    pad = LANE - dq
    qa, qb, wk, wv = [], [], [], []
    for h in range(N_HEADS):
        nope = w_uq[..., h * dq:h * dq + MLA_NOPE]
        rope = w_uq[..., h * dq + MLA_NOPE:(h + 1) * dq]
        qa += [nope, rope, pad]
        qb += [MLA_NOPE, _rot_cols(rope), pad]
        wk += [w_ukv[..., h * 2 * HEAD_DIM:h * 2 * HEAD_DIM + MLA_NOPE], LANE - MLA_NOPE]
        wv += [w_ukv[..., h * 2 * HEAD_DIM + MLA_NOPE:(h + 1) * 2 * HEAD_DIM]]
    return tuple(_cat(p, like).astype(BF16) for p, like in ((qa, w_uq), (qb, w_uq), (wk, w_ukv), (wv, w_ukv)))


def _rope_tables(s):
    inv = ROPE_THETA ** (-jnp.arange(0, MLA_ROPE, 2, dtype=F32) / MLA_ROPE)
    ang = jnp.arange(s, dtype=F32)[:, None] * inv[None, :]
    cos, sin = jnp.cos(ang), jnp.sin(ang)
    pad = LANE - MLA_NOPE - MLA_ROPE
    cosq = _cat([jnp.ones((s, MLA_NOPE), F32), cos, cos, pad], cos)
    sinq = _cat([MLA_NOPE, sin, sin, pad], cos)
    cosk = _cat([MLA_NOPE, cos, cos, pad], cos)
    return cosq, sinq, cosk


def _tile_heads(g):
    return jnp.tile(g, N_HEADS)[None, :]


def _pick(n, prefs):
    for p in prefs:
        if n % p == 0:
            return p
    return n


def kernel(x, w_in, w_out, pre_mix_g, post_mix_g, pre_ffn_g, post_ffn_g, hgrn_lb_logits, hgrn_norm_g,
           gla_w_gk2, gla_b_gk, gla_norm_g, mla_q_norm_g, mla_w_uq, mla_kv_norm_g, mla_w_ukv,
           gdn_conv_w, gdn_a_log, gdn_dt_bias, gdn_norm_g, ffn_w_gate, ffn_w_up, ffn_conv_w, ffn_w_down):
    bsz, s, d = x.shape
    depth = w_in.shape[0]
    t = bsz * s
    tm = _pick(t, (512, 256))
    tt = _pick(s, (256,))
    ts = _pick(s, (512, 256))
    tq = _pick(s, (2048, 1024, 512, 256, 128))
    fb = _pick(ffn_w_gate.shape[2], (512, 256))

    cosq, sinq, cosk = _rope_tables(s)
    p = jax.nn.softmax(hgrn_lb_logits.astype(F32), axis=0)
    lower_bounds = jnp.cumsum(p, axis=0) - p[0]

    lanes = jnp.arange(LANE)
    head_of = jnp.arange(GROUP_WIDTH) // HEAD_DIM
    ebeta = (lanes[:, None] == M_BETA + head_of[None, :]).astype(BF16)
    ea = (lanes[:, None] == M_A + head_of[None, :]).astype(BF16)

    w_in_all, widths = _in_proj_layout(w_in)
    wqa, wqb, wk, wv = _mla_layout(mla_w_uq, mla_w_ukv)
    wgk = jnp.concatenate([gla_w_gk2, jnp.zeros((depth, LANE - GLA_GATE_RANK, N_HEADS * DK_B), F32)],
                          axis=1).astype(BF16)
    a_tail = LANE - M_A - N_HEADS
    apad = _cat([M_A, jnp.exp(gdn_a_log.astype(F32)), a_tail], gdn_a_log)
    dtpad = _cat([M_A, gdn_dt_bias, a_tail], gdn_dt_bias)
    w_out16, wg16, wu16, wd16 = (a.astype(BF16) for a in (w_out, ffn_w_gate, ffn_w_up, ffn_w_down))

    for l in range(depth):
        za, zb, zc, zd, zm, zm2 = _in_proj(x.reshape(t, d), pre_mix_g[l][None, :], w_in_all, l, widths, tm)
        r3 = lambda a: a.reshape(bsz, s, a.shape[-1])
        za, zb, zc, zd, zm, zm2 = map(r3, (za, zb, zc, zd, zm, zm2))

        qt, kh, vt = _mla_pre(zc, zm, zm2, mla_q_norm_g[l][None, :], mla_kv_norm_g[l][None, :],
                              wqa[l], wqb[l], wk[l], wv[l], cosq, sinq, cosk, tq)
        o_c = _mla_attn(qt, kh, vt, tq)
        o_a, o_b, o_d = _mixers(
            za, zb, zd, zm,
            (lower_bounds[l][None, :], _tile_heads(hgrn_norm_g[l])),
            (wgk[l], gla_b_gk[l][None, :], _tile_heads(gla_norm_g[l])),
            (gdn_conv_w[l], apad[l][None, :], dtpad[l][None, :], ebeta, ea, _tile_heads(gdn_norm_g[l])), tt)

        x = _ffn((o_a, o_b, o_c, o_d), w_out16, post_mix_g[l][None, :], x,
                 pre_ffn_g[l][None, :], wg16, wu16, ffn_conv_w[l], wd16, post_ffn_g[l][None, :],
                 layer=l, tm=ts, fb=fb)
    return x
```
